```python
import functools
import jax, jax.numpy as jnp
from jax import lax
import numpy as np

D_MODEL = 1024
BATCH = 1
SEQ = 16384
DEPTH = 1
DEC_BATCH = 128
DEC_SEQ = 4
PAST_LEN = 8192
PAGE_SIZE = 128

N_HEADS = 8
HEAD_DIM = 64
ATT_W = N_HEADS * HEAD_DIM
RW_HEADS = 8
RW_N = 64
RW_W = RW_HEADS * RW_N
DECAY_LORA = 64
AAA_LORA = 64
GATE_LORA = 128
D_FF = 4 * D_MODEL
MOBA_BLOCK = 256
MOBA_TOPK = 3
Q_BLOCK = 128
ROPE_THETA = 10000.0
NORM_EPS = 1e-6
GN_EPS = 64e-5
NEG_INF = -1e30
N_BRANCH = 2
SHIFT_W = 3 * RW_W + DECAY_LORA + AAA_LORA + GATE_LORA
IN_W = 3 * ATT_W + SHIFT_W + N_BRANCH * D_MODEL

kernel_name = 'moba_rwkv7_parallel_hybrid_step'

F32 = jnp.float32


def rmsnorm(x, g):
    xf = x.astype(F32)
    y = xf * lax.rsqrt(jnp.mean(xf * xf, axis=-1, keepdims=True) + NORM_EPS) * g.astype(F32)
    return y.astype(x.dtype)


def rope(x, pos):
    half = HEAD_DIM // 2
    inv = ROPE_THETA ** (-jnp.arange(half, dtype=F32) / half)
    ang = pos.astype(F32)[:, None] * inv[None, :]
    cos = jnp.cos(ang)[None, :, None, :]
    sin = jnp.sin(ang)[None, :, None, :]
    xf = x.astype(F32)
    x1, x2 = xf[..., :half], xf[..., half:]
    return jnp.concatenate([x1 * cos - x2 * sin, x2 * cos + x1 * sin], axis=-1).astype(x.dtype)


def moba_mix(q, k_own, v_own, own_mask, k_sel=None, v_sel=None, sel_ok=None):
    qf = q.astype(F32) * (HEAD_DIM ** -0.5)
    s_own = jnp.einsum('bhqd,bhkd->bhqk', qf, k_own.astype(F32))
    s_own = jnp.where(own_mask, s_own, NEG_INF)
    if k_sel is None:
        p = jax.nn.softmax(s_own, axis=-1)
        return jnp.einsum('bhqk,bhkd->bhqd', p, v_own.astype(F32)).astype(q.dtype)
    s_sel = jnp.einsum('bhqd,bhqtkd->bhqtk', qf, k_sel.astype(F32))
    if sel_ok is not None:
        s_sel = jnp.where(sel_ok[..., None], s_sel, NEG_INF)
    b, h, nq, t, blk = s_sel.shape
    s = jnp.concatenate([s_sel.reshape(b, h, nq, t * blk), s_own], axis=-1)
    p = jax.nn.softmax(s, axis=-1)
    p_sel = p[..., :t * blk].reshape(b, h, nq, t, blk)
    out = jnp.einsum('bhqtk,bhqtkd->bhqd', p_sel, v_sel.astype(F32)) + jnp.einsum('bhqk,bhkd->bhqd', p[..., t * blk:], v_own.astype(F32))
    return out.astype(q.dtype)


def moba_prompt(q, k, v):
    B, S = q.shape[:2]
    n_blk = -(-S // MOBA_BLOCK)
    pad = n_blk * MOBA_BLOCK - S

    def blocks(t):
        t = jnp.pad(t, ((0, 0), (0, pad), (0, 0), (0, 0)))
        return t.reshape(B, n_blk, MOBA_BLOCK, N_HEADS, HEAD_DIM).transpose(0, 3, 1, 2, 4)

    kb, vb = blocks(k), blocks(v)
    blk_mean = jnp.mean(kb, axis=3, dtype=F32)
    topk = min(MOBA_TOPK, n_blk - 1)
    n_chunk = S // Q_BLOCK
    qc = q.reshape(B, n_chunk, Q_BLOCK, N_HEADS, HEAD_DIM).transpose(1, 0, 3, 2, 4)
    bi = jnp.arange(B)[:, None, None, None]
    hi = jnp.arange(N_HEADS)[None, :, None, None]

    def chunk(args):
        c, qq = args
        q_pos = c * Q_BLOCK + jnp.arange(Q_BLOCK)
        ob = (c * Q_BLOCK) // MOBA_BLOCK
        k_own = lax.dynamic_index_in_dim(kb, ob, axis=2, keepdims=False)
        v_own = lax.dynamic_index_in_dim(vb, ob, axis=2, keepdims=False)
        own_mask = (ob * MOBA_BLOCK + jnp.arange(MOBA_BLOCK))[None, :] <= q_pos[:, None]
        if topk == 0:
            return moba_mix(qq, k_own, v_own, own_mask)
        s_blk = jnp.einsum('bhqd,bhnd->bhqn', qq.astype(F32), blk_mean)
        s_blk = jnp.where(jnp.arange(n_blk) < ob, s_blk, NEG_INF)
        _, sel = lax.top_k(s_blk, topk)
        return moba_mix(qq, k_own, v_own, own_mask, kb[bi, hi, sel], vb[bi, hi, sel], sel < ob)

    out = lax.map(chunk, (jnp.arange(n_chunk), qc))
    return out.transpose(1, 0, 3, 2, 4).reshape(B, S, ATT_W)


def moba_sample(q, k, v, cache_k, cache_v, page_table, layer):
    DB, Q = q.shape[:2]
    n_pages = page_table.shape[1]
    past = n_pages * PAGE_SIZE
    ppb = MOBA_BLOCK // PAGE_SIZE
    n_full = past // MOBA_BLOCK
    own_pages = (past % MOBA_BLOCK) // PAGE_SIZE
    qt, kt, vt = (t.transpose(0, 2, 1, 3) for t in (q, k, v))
    causal = jnp.arange(Q)[None, :] <= jnp.arange(Q)[:, None]
    if own_pages > 0:
        pg = page_table[:, n_pages - own_pages:]

        def own_rows(cache):
            rows = cache[pg, layer]
            return rows.transpose(0, 2, 1, 3, 4).reshape(DB, N_HEADS, own_pages * PAGE_SIZE, HEAD_DIM).astype(k.dtype)

        k_own = jnp.concatenate([own_rows(cache_k), kt], axis=2)
        v_own = jnp.concatenate([own_rows(cache_v), vt], axis=2)
        own_mask = jnp.concatenate([jnp.ones((Q, own_pages * PAGE_SIZE), dtype=bool), causal], axis=1)
    else:
        k_own, v_own, own_mask = kt, vt, causal
    topk = min(MOBA_TOPK, n_full)
    if topk == 0:
        out = moba_mix(qt, k_own, v_own, own_mask)
    else:
        page_mean = jnp.mean(cache_k, axis=3, dtype=F32)[:, layer]
        blk_mean = page_mean[page_table[:, :n_full * ppb]].reshape(DB, n_full, ppb, N_HEADS, HEAD_DIM).mean(axis=2)
        s_blk = jnp.einsum('bhqd,bnhd->bhqn', qt.astype(F32), blk_mean)
        _, sel = lax.top_k(s_blk, topk)
        logical = sel[..., None] * ppb + jnp.arange(ppb)
        phys = page_table[jnp.arange(DB)[:, None, None, None, None], logical]
        hidx = jnp.arange(N_HEADS)[None, :, None, None, None]
        k_sel = cache_k[phys, layer, hidx].reshape(DB, N_HEADS, Q, topk, MOBA_BLOCK, HEAD_DIM)
        v_sel = cache_v[phys, layer, hidx].reshape(DB, N_HEADS, Q, topk, MOBA_BLOCK, HEAD_DIM)
        out = moba_mix(qt, k_own, v_own, own_mask, k_sel, v_sel)
    return out.transpose(0, 2, 1, 3).reshape(DB, Q, ATT_W)


def wkv_scan(state, r, w, k, v, a, b):
    def step(S, inp):
        rt, wt, kt, vt, at, bt = inp
        sa = jnp.einsum('bhvk,bhk->bhv', S, at)
        S = S * wt[:, :, None, :] + sa[..., None] * bt[:, :, None, :] + vt[..., None] * kt[:, :, None, :]
        return S, jnp.einsum('bhvk,bhk->bhv', S, rt)
    xs = tuple(t.transpose(1, 0, 2, 3) for t in (r, w, k, v, a, b))
    S, ys = lax.scan(step, state, xs)
    return S, ys.transpose(1, 0, 2, 3)


def rwkv7_branch(rw, shift_prev, wkv0, p):
    B, T, _ = rw.shape
    prev = jnp.concatenate([shift_prev[:, None].astype(rw.dtype), rw[:, :-1]], axis=1)
    m = rw + (prev - rw) * p['rw_mu']
    r, k, v, wd, ad, gd = jnp.split(m, [RW_W, 2 * RW_W, 3 * RW_W, 3 * RW_W + DECAY_LORA, 3 * RW_W + DECAY_LORA + AAA_LORA], axis=-1)

    def heads(t):
        return t.astype(F32).reshape(B, T, RW_HEADS, RW_N)

    w_raw = (p['rw_w0'] + jnp.tanh(wd) @ p['rw_w2']).astype(F32)
    decay = jnp.exp(-jnp.exp(-jax.nn.softplus(-w_raw) - 0.5))
    a = jax.nn.sigmoid((p['rw_a0'] + ad @ p['rw_a2']).astype(F32))
    g = (jax.nn.sigmoid(gd) @ p['rw_g2']).astype(F32)
    kf = k.astype(F32)
    kk = heads(kf * p['rw_kk'])
    kk = kk / jnp.maximum(jnp.sqrt(jnp.sum(kk * kk, axis=-1, keepdims=True)), 1e-12)
    kf = kf * (1.0 + (a - 1.0) * p['rw_ka'])
    rh, kh, vh, ah = heads(r), heads(kf), heads(v), heads(a)
    S, o = wkv_scan(wkv0.astype(F32), rh, heads(decay), kh, vh, -kk, kk * ah)
    mu = jnp.mean(o, axis=-1, keepdims=True)
    var = jnp.mean(jnp.square(o - mu), axis=-1, keepdims=True)
    o = ((o - mu) * lax.rsqrt(var + GN_EPS)).reshape(B, T, RW_W) * p['rw_ln_w'] + p['rw_ln_b']
    bonus = jnp.sum(rh * kh * p['rw_rk'].astype(F32), axis=-1, keepdims=True) * vh
    o = (o + bonus.reshape(B, T, RW_W)) * g
    return o.astype(rw.dtype), S.astype(wkv0.dtype)


def decoder_layer(x, pos, attend, shift_prev, wkv0, p):
    B, T, _ = x.shape
    h = rmsnorm(x, p['n1_pre'])
    proj = h @ p['w_in']
    q, k, v, rw, gates = jnp.split(proj, [ATT_W, 2 * ATT_W, 3 * ATT_W, 3 * ATT_W + SHIFT_W], axis=-1)
    q = rope(q.reshape(B, T, N_HEADS, HEAD_DIM), pos)
    k = rope(k.reshape(B, T, N_HEADS, HEAD_DIM), pos)
    v = v.reshape(B, T, N_HEADS, HEAD_DIM)
    y_att = attend(q, k, v)
    y_rw, wkv_new = rwkv7_branch(rw, shift_prev, wkv0, p)
    g = jax.nn.sigmoid(gates.astype(F32)).astype(x.dtype)
    g_att, g_rw = g[..., :D_MODEL], g[..., D_MODEL:]
    merged = g_att * (y_att @ p['w_att_br']) + g_rw * (y_rw @ p['w_rw_br'])
    x = x + rmsnorm(merged @ p['w_out'], p['n1_post'])
    u = jnp.square(jax.nn.relu(rmsnorm(x, p['n2_pre']) @ p['w_up']))
    x = x + rmsnorm(u @ p['w_down'], p['n2_post'])
    return x, k.transpose(0, 2, 1, 3), v.transpose(0, 2, 1, 3), wkv_new, rw[:, -1]


def setup_inputs(seed: int = 0) -> dict:
    key = jax.random.key(seed)
    ks = jax.random.split(key, 32)
    n_pages = PAST_LEN // PAGE_SIZE
    n_pool = (DEC_BATCH * n_pages * 5) // 4
    nrm = lambda i, shape, s: jax.random.normal(ks[i], shape, F32) * s
    page_table = jax.random.permutation(ks[0], n_pool)[: DEC_BATCH * n_pages].reshape(DEC_BATCH, n_pages).astype(jnp.int32)
    return {
        'x_prompt': nrm(1, (BATCH, SEQ, D_MODEL), 1.0),
        'x_sample': nrm(2, (DEC_BATCH, DEC_SEQ, D_MODEL), 1.0),
        'cache_k': nrm(3, (n_pool, DEPTH, N_HEADS, PAGE_SIZE, HEAD_DIM), 1.0),
        'cache_v': nrm(4, (n_pool, DEPTH, N_HEADS, PAGE_SIZE, HEAD_DIM), 1.0),
        'state_wkv': nrm(5, (DEC_BATCH, DEPTH, RW_HEADS, RW_N, RW_N), 0.3),
        'state_shift': nrm(6, (DEC_BATCH, DEPTH, SHIFT_W), 1.0),
        'page_table': page_table,
        'n1_pre': 1.0 + nrm(7, (DEPTH, D_MODEL), 0.05),
        'n1_post': 1.0 + nrm(8, (DEPTH, D_MODEL), 0.05),
        'n2_pre': 1.0 + nrm(9, (DEPTH, D_MODEL), 0.05),
        'n2_post': 1.0 + nrm(10, (DEPTH, D_MODEL), 0.05),
        'w_in': nrm(11, (DEPTH, D_MODEL, IN_W), D_MODEL ** -0.5),
        'rw_mu': jax.random.uniform(ks[12], (DEPTH, SHIFT_W), F32),
        'rw_w0': nrm(13, (DEPTH, RW_W), 0.5),
        'rw_w2': nrm(14, (DEPTH, DECAY_LORA, RW_W), 0.1),
        'rw_a0': nrm(15, (DEPTH, RW_W), 0.5),
        'rw_a2': nrm(16, (DEPTH, AAA_LORA, RW_W), 0.5 * AAA_LORA ** -0.5),
        'rw_g2': nrm(17, (DEPTH, GATE_LORA, RW_W), GATE_LORA ** -0.5),
        'rw_kk': 0.85 + nrm(18, (DEPTH, RW_W), 0.05),
        'rw_ka': 1.0 + nrm(19, (DEPTH, RW_W), 0.05),
        'rw_rk': nrm(20, (DEPTH, RW_HEADS, RW_N), 0.1),
        'rw_ln_w': 1.0 + nrm(21, (DEPTH, RW_W), 0.1),
        'rw_ln_b': nrm(22, (DEPTH, RW_W), 0.01),
        'w_att_br': nrm(23, (DEPTH, ATT_W, D_MODEL), ATT_W ** -0.5),
        'w_rw_br': nrm(24, (DEPTH, RW_W, D_MODEL), RW_W ** -0.5),
        'w_out': nrm(25, (DEPTH, D_MODEL, D_MODEL), D_MODEL ** -0.5),
        'w_up': nrm(26, (DEPTH, D_MODEL, D_FF), D_MODEL ** -0.5),
        'w_down': nrm(27, (DEPTH, D_FF, D_MODEL), D_FF ** -0.5),
    }


def reference(x_prompt, x_sample, cache_k, cache_v, state_wkv, state_shift, page_table, n1_pre, n1_post, n2_pre, n2_post, w_in, rw_mu, rw_w0, rw_w2, rw_a0, rw_a2, rw_g2, rw_kk, rw_ka, rw_rk, rw_ln_w, rw_ln_b, w_att_br, w_rw_br, w_out, w_up, w_down):
    B, S = x_prompt.shape[:2]
    DB, Q = x_sample.shape[:2]
    past = page_table.shape[1] * PAGE_SIZE
    pos_p = jnp.arange(S)
    pos_s = past + jnp.arange(Q)
    yp, ys = x_prompt, x_sample
    kp_l, vp_l, ks_l, vs_l, wp_l, ws_l, hp_l, hs_l = [], [], [], [], [], [], [], []
    for l in range(DEPTH):
        p = dict(n1_pre=n1_pre[l], n1_post=n1_post[l], n2_pre=n2_pre[l], n2_post=n2_post[l], w_in=w_in[l],
                 rw_mu=rw_mu[l], rw_w0=rw_w0[l], rw_w2=rw_w2[l], rw_a0=rw_a0[l], rw_a2=rw_a2[l], rw_g2=rw_g2[l],
                 rw_kk=rw_kk[l], rw_ka=rw_ka[l], rw_rk=rw_rk[l], rw_ln_w=rw_ln_w[l], rw_ln_b=rw_ln_b[l],
                 w_att_br=w_att_br[l], w_rw_br=w_rw_br[l], w_out=w_out[l], w_up=w_up[l], w_down=w_down[l])
        yp, kp, vp, wp, hp = decoder_layer(yp, pos_p, moba_prompt, jnp.zeros((B, SHIFT_W), yp.dtype),
                                           jnp.zeros((B, RW_HEADS, RW_N, RW_N), yp.dtype), p)
        attend_s = functools.partial(moba_sample, cache_k=cache_k, cache_v=cache_v, page_table=page_table, layer=l)
        ys, ks_, vs_, ws, hs = decoder_layer(ys, pos_s, attend_s, state_shift[:, l], state_wkv[:, l], p)
        kp_l.append(kp); vp_l.append(vp); ks_l.append(ks_); vs_l.append(vs_)
        wp_l.append(wp); ws_l.append(ws); hp_l.append(hp); hs_l.append(hs)
    return (yp, ys, jnp.stack(kp_l, axis=1), jnp.stack(vp_l, axis=1), jnp.stack(ks_l, axis=1), jnp.stack(vs_l, axis=1), jnp.stack(wp_l, axis=1), jnp.stack(ws_l, axis=1), jnp.stack(hp_l, axis=1), jnp.stack(hs_l, axis=1))
```

```python
import functools
import math

import jax
import jax.numpy as jnp
from jax import lax
from jax.experimental import pallas as pl
from jax.experimental.pallas import tpu as pltpu

F32 = jnp.float32
BF16 = jnp.bfloat16

D_MODEL = 1024
N_HEADS = 8
HEAD_DIM = 64
ATT_W = N_HEADS * HEAD_DIM
RW_HEADS = 8
RW_N = 64
RW_W = RW_HEADS * RW_N
DECAY_LORA = 64
AAA_LORA = 64
GATE_LORA = 128
D_FF = 4 * D_MODEL
MOBA_BLOCK = 256
MOBA_TOPK = 3
Q_BLOCK = 128
PAGE_SIZE = 128
ROPE_THETA = 10000.0
NORM_EPS = 1e-6
GN_EPS = 64e-5
NEG_INF = -1e30
SHIFT_W = 3 * RW_W + DECAY_LORA + AAA_LORA + GATE_LORA
GATE_W = 2 * D_MODEL
IN_W = 3 * ATT_W + SHIFT_W + GATE_W

LANES = 128
VMEM_LIMIT = 56 * 1024 * 1024

NT_DIMS = (((1,), (1,)), ((), ()))
TN_DIMS = (((0,), (0,)), ((), ()))


def _split2(x):
    hi = x.astype(BF16)
    lo = (x - hi.astype(F32)).astype(BF16)
    return hi, lo


def _dot3(a, b, dims=(((1,), (0,)), ((), ()))):
    ah, al = _split2(a)
    bh, bl = _split2(b)
    d = functools.partial(lax.dot_general, dimension_numbers=dims, preferred_element_type=F32)
    return d(ah, bh) + (d(ah, bl) + d(al, bh))


def _split3(x):
    x1 = x.astype(BF16)
    r1 = x - x1.astype(F32)
    x2 = r1.astype(BF16)
    x3 = (r1 - x2.astype(F32)).astype(BF16)
    return x1, x2, x3


def _dot_exact_lhs(mask, b):
    a = mask.astype(BF16)
    b1, b2, b3 = _split3(b)
    d = functools.partial(jnp.dot, preferred_element_type=F32)
    return d(a, b1) + (d(a, b2) + d(a, b3))


def _dot_exact_rhs(a, mask):
    a1, a2, a3 = _split3(a)
    o = mask.astype(BF16)
    d = functools.partial(jnp.dot, preferred_element_type=F32)
    return d(a1, o) + (d(a2, o) + d(a3, o))


def _sigmoid(x):
    return 1.0 / (1.0 + jnp.exp(-x))


def _proj_kernel(x_ref, cos_ref, sin_ref, g_ref, w_ref,
                 q_ref, kb_ref, vtb_ref, kt_ref, vt_ref, rw_ref, gate_ref, kmean_ref, *, tm):
    x = x_ref[...]
    ms = jnp.mean(x * x, axis=-1, keepdims=True)
    h = (x * lax.rsqrt(ms + NORM_EPS) * g_ref[...]).astype(BF16)
    cos = cos_ref[...]
    sin = sin_ref[...]
    lane = lax.broadcasted_iota(jnp.int32, (tm, LANES), 1)
    first_half = (lane % HEAD_DIM) < (HEAD_DIM // 2)

    def rope(p):
        sw = jnp.where(first_half, pltpu.roll(p, LANES - HEAD_DIM // 2, 1), pltpu.roll(p, HEAD_DIM // 2, 1))
        return p * cos + sw * sin

    def proj(lo, width):
        return jnp.dot(h, w_ref[:, lo:lo + width], preferred_element_type=F32)

    pq = proj(0, ATT_W)
    for c in range(ATT_W // LANES):
        q_ref[:, c * LANES:(c + 1) * LANES] = rope(pq[:, c * LANES:(c + 1) * LANES])
    pk = proj(ATT_W, ATT_W)
    for c in range(ATT_W // LANES):
        cols = slice(c * LANES, (c + 1) * LANES)
        kr = rope(pk[:, cols])
        kb_ref[:, cols] = kr.astype(BF16)
        kt_ref[cols, :] = kr.T
        kmean_ref[0, :, cols] = jnp.sum(kr, axis=0, keepdims=True) * (1.0 / tm)
    pv = proj(2 * ATT_W, ATT_W)
    for c in range(ATT_W // LANES):
        cols = slice(c * LANES, (c + 1) * LANES)
        vt = pv[:, cols].T
        vt_ref[cols, :] = vt
        vtb_ref[0, cols, :] = vt.astype(BF16)
    rw_ref[...] = proj(3 * ATT_W, SHIFT_W)
    gate_ref[...] = _sigmoid(proj(3 * ATT_W + SHIFT_W, GATE_W))


def _rope_tables(pos):
    half = HEAD_DIM // 2
    inv = ROPE_THETA ** (-jnp.arange(half, dtype=F32) / half)
    ang = pos.astype(F32)[:, None] * inv[None, :]
    cos = jnp.cos(ang)
    sin = jnp.sin(ang)
    return jnp.tile(jnp.concatenate([cos, cos], axis=1), (1, 2)), jnp.tile(jnp.concatenate([-sin, sin], axis=1), (1, 2))


def _project(x, cos, sin, g, w_in_bf16):
    rows = x.shape[0]
    tm = MOBA_BLOCK
    n = rows // tm
    row = lambda w: pl.BlockSpec((tm, w), lambda i: (i, 0))
    const = lambda a, b: pl.BlockSpec((a, b), lambda i: (0, 0))
    col = pl.BlockSpec((ATT_W, tm), lambda i: (0, i))
    return pl.pallas_call(
        functools.partial(_proj_kernel, tm=tm),
        grid=(n,),
        in_specs=[row(D_MODEL), row(LANES), row(LANES), const(1, D_MODEL), const(D_MODEL, IN_W)],
        out_specs=[row(ATT_W), row(ATT_W), pl.BlockSpec((1, ATT_W, tm), lambda i: (i, 0, 0)), col, col,
                   row(SHIFT_W), row(GATE_W), pl.BlockSpec((1, 1, ATT_W), lambda i: (i, 0, 0))],
        out_shape=[
            jax.ShapeDtypeStruct((rows, ATT_W), F32),
            jax.ShapeDtypeStruct((rows, ATT_W), BF16),
            jax.ShapeDtypeStruct((n, ATT_W, tm), BF16),
            jax.ShapeDtypeStruct((ATT_W, rows), F32),
            jax.ShapeDtypeStruct((ATT_W, rows), F32),
            jax.ShapeDtypeStruct((rows, SHIFT_W), F32),
            jax.ShapeDtypeStruct((rows, GATE_W), F32),
            jax.ShapeDtypeStruct((n, 1, ATT_W), F32),
        ],
        compiler_params=pltpu.CompilerParams(dimension_semantics=("arbitrary",), vmem_limit_bytes=VMEM_LIMIT),
        name="project",
    )(x, cos, sin, g, w_in_bf16)


def _moba_prompt_kernel(q_ref, k_ref, vt_ref, kmean_ref, y_ref, sel_ref, *, n_blk):
    c = pl.program_id(1)
    ob = (c * Q_BLOCK) // MOBA_BLOCK
    q = q_ref[...]
    lane = lax.broadcasted_iota(jnp.int32, (Q_BLOCK, LANES), 1)
    kmean = kmean_ref[...]
    blk = lax.broadcasted_iota(jnp.int32, (n_blk, Q_BLOCK), 0)
    qs = []
    for hh in range(2):
        qh = jnp.where((lane // HEAD_DIM) == hh, q, 0.0)
        qs.append((qh * (HEAD_DIM ** -0.5)).astype(BF16))
        sb = _dot3(kmean, qh, NT_DIMS)
        sb = jnp.where(blk < ob, sb, NEG_INF)
        sel = jnp.zeros((n_blk, Q_BLOCK), jnp.bool_)
        for _ in range(MOBA_TOPK):
            m = jnp.max(sb, axis=0, keepdims=True)
            idx = jnp.min(jnp.where(sb == m, blk, n_blk), axis=0, keepdims=True)
            pick = blk == idx
            sel = jnp.logical_or(sel, pick)
            sb = jnp.where(pick, -jnp.inf, sb)
        sel = jnp.logical_and(sel, blk < ob)
        sel_ref[:, hh * Q_BLOCK:(hh + 1) * Q_BLOCK] = sel.astype(F32)
    qs = jnp.concatenate(qs, axis=0)

    def scores(j):
        return lax.dot_general(k_ref[j], qs, NT_DIMS, preferred_element_type=F32)

    s = scores(ob)
    key = lax.broadcasted_iota(jnp.int32, (MOBA_BLOCK, 2 * Q_BLOCK), 0)
    qi = lax.broadcasted_iota(jnp.int32, (MOBA_BLOCK, 2 * Q_BLOCK), 1) % Q_BLOCK
    s = jnp.where(key <= (c % 2) * Q_BLOCK + qi, s, NEG_INF)
    m0 = jnp.max(s, axis=0, keepdims=True)
    p = jnp.exp(s - m0)
    l0 = jnp.sum(p, axis=0, keepdims=True)
    acc0 = jnp.dot(vt_ref[ob], p.astype(BF16), preferred_element_type=F32)

    def body(j, carry):
        m, l, acc = carry
        s = scores(j)
        flag = sel_ref[pl.ds(j, 1), :]
        s = jnp.where(flag > 0.0, s, NEG_INF)
        m_new = jnp.maximum(m, jnp.max(s, axis=0, keepdims=True))
        alpha = jnp.exp(m - m_new)
        p = jnp.exp(s - m_new)
        l = alpha * l + jnp.sum(p, axis=0, keepdims=True)
        acc = alpha * acc + jnp.dot(vt_ref[j], p.astype(BF16), preferred_element_type=F32)
        return m_new, l, acc

    m, l, acc = lax.fori_loop(0, ob, body, (m0, l0, acc0))
    o = acc * (1.0 / l)
    row = lax.broadcasted_iota(jnp.int32, (LANES, Q_BLOCK), 0)
    ot = jnp.where(row < HEAD_DIM, o[:, :Q_BLOCK], o[:, Q_BLOCK:])
    y_ref[...] = ot.T.astype(y_ref.dtype)


def _moba_prompt(q, k_blocks, vt_blocks, kmean):
    s = q.shape[0]
    n_blk = k_blocks.shape[0]
    n_chunk = s // Q_BLOCK
    return pl.pallas_call(
        functools.partial(_moba_prompt_kernel, n_blk=n_blk),
        grid=(ATT_W // LANES, n_chunk),
        in_specs=[
            pl.BlockSpec((Q_BLOCK, LANES), lambda hp, c: (c, hp)),
            pl.BlockSpec((n_blk, MOBA_BLOCK, LANES), lambda hp, c: (0, 0, hp)),
            pl.BlockSpec((n_blk, LANES, MOBA_BLOCK), lambda hp, c: (0, hp, 0)),
            pl.BlockSpec((n_blk, LANES), lambda hp, c: (0, hp)),
        ],
        out_specs=pl.BlockSpec((Q_BLOCK, LANES), lambda hp, c: (c, hp)),
        out_shape=jax.ShapeDtypeStruct((s, ATT_W), BF16),
        scratch_shapes=[pltpu.VMEM((n_blk, 2 * Q_BLOCK), F32)],
        compiler_params=pltpu.CompilerParams(dimension_semantics=("arbitrary", "arbitrary"),
                                             vmem_limit_bytes=VMEM_LIMIT),
        name="moba_prompt",
    )(q, k_blocks, vt_blocks, kmean)


def _rwkv_kernel(rw_ref, prev_ref, s0_ref, mu_ref, w0_ref, w2_ref, a0_ref, a2_ref, g2_ref, kk_ref, ka_ref,
                 rk_ref, lnw_ref, lnb_ref, y_ref, s_ref, *, chunk, t_valid):
    ci = pl.program_id(1)

    @pl.when(ci == 0)
    def _():
        s_ref[...] = s0_ref[...]

    rw = rw_ref[...]
    m = rw + (prev_ref[...] - rw) * mu_ref[...]
    r = m[:, 0:RW_W]
    k = m[:, RW_W:2 * RW_W]
    v = m[:, 2 * RW_W:3 * RW_W]
    o1 = 3 * RW_W
    wd = m[:, o1:o1 + DECAY_LORA]
    ad = m[:, o1 + DECAY_LORA:o1 + DECAY_LORA + AAA_LORA]
    gd = m[:, o1 + DECAY_LORA + AAA_LORA:]
    bdot = lambda a, b: jnp.dot(a.astype(BF16), b.astype(BF16), preferred_element_type=F32)
    w_raw = w0_ref[...] + bdot(jnp.tanh(wd), w2_ref[...])
    logw = -math.exp(-0.5) * _sigmoid(w_raw)
    a_sig = _sigmoid(a0_ref[...] + bdot(ad, a2_ref[...]))
    g = bdot(_sigmoid(gd), g2_ref[...])

    li = lax.broadcasted_iota(jnp.int32, (RW_W, RW_W), 0) // RW_N
    lj = lax.broadcasted_iota(jnp.int32, (RW_W, RW_W), 1) // RW_N
    head_ones = (li == lj).astype(F32)

    kk = k * kk_ref[...]
    kk_norm = jnp.sqrt(_dot_exact_rhs(kk * kk, head_ones))
    kk = kk / jnp.maximum(kk_norm, 1e-12)
    k2 = k * (1.0 + (a_sig - 1.0) * ka_ref[...])
    a_vec = -kk
    b_vec = kk * a_sig

    if t_valid < chunk:
        valid = lax.broadcasted_iota(jnp.int32, (chunk, RW_W), 0) < t_valid
        zero = lambda t: jnp.where(valid, t, 0.0)
        logw, a_vec, b_vec, k2s, vs = zero(logw), zero(a_vec), zero(b_vec), zero(k2), zero(v)
    else:
        k2s, vs = k2, v

    ti = lax.broadcasted_iota(jnp.int32, (chunk, chunk), 0)
    tj = lax.broadcasted_iota(jnp.int32, (chunk, chunk), 1)
    lower_incl = tj <= ti
    lower_strict = tj < ti
    cs = _dot_exact_lhs(lower_incl.astype(F32), logw)
    p_incl = jnp.exp(cs)
    p_inv = jnp.exp(-cs)
    a_t = a_vec * jnp.exp(cs - logw)
    r_t = r * p_incl
    b_t = b_vec * p_inv
    k_t = k2s * p_inv
    p_last = p_incl[chunk - 1:chunk, :]
    eye = (ti == tj).astype(F32)

    n_double = max(int(math.log2(chunk)) - 1, 0)
    ys = []
    for h in range(RW_HEADS):
        sl = slice(h * RW_N, (h + 1) * RW_N)
        ar = jnp.concatenate([a_t[:, sl], r_t[:, sl]], axis=0)
        bk = jnp.concatenate([b_t[:, sl], k_t[:, sl]], axis=0)
        gram = _dot3(ar, bk, NT_DIMS)
        l_ab = jnp.where(lower_strict, gram[:chunk, :chunk], 0.0)
        l_ak = jnp.where(lower_strict, gram[:chunk, chunk:], 0.0)
        m_rb = jnp.where(lower_incl, gram[chunk:, :chunk], 0.0)
        m_rk = jnp.where(lower_incl, gram[chunk:, chunk:], 0.0)
        x = eye + l_ab
        pw = l_ab
        for _ in range(n_double):
            pw = _dot3(pw, pw)
            x = x + _dot3(pw, x)
        s0 = s_ref[0, h]
        ars = _dot3(ar, s0, NT_DIMS)
        vh = vs[:, sl]
        u = _dot3(x, ars[:chunk] + _dot3(l_ak, vh))
        uv = jnp.concatenate([u, vh], axis=0)
        ys.append(ars[chunk:] + _dot3(jnp.concatenate([m_rb, m_rk], axis=1), uv))
        s_ref[0, h] = (s0 + _dot3(uv, bk, TN_DIMS)) * p_last[:, sl]
    o = jnp.concatenate(ys, axis=1)

    mean = _dot_exact_rhs(o, head_ones) * (1.0 / RW_N)
    d = o - mean
    var = _dot_exact_rhs(d * d, head_ones) * (1.0 / RW_N)
    o = d * lax.rsqrt(var + GN_EPS) * lnw_ref[...] + lnb_ref[...]
    bonus = _dot_exact_rhs(r * k2 * rk_ref[...], head_ones) * v
    y_ref[...] = ((o + bonus) * g).astype(y_ref.dtype)


def _rwkv(rw, prev, s0, p, *, n_batch, chunk, t_valid):
    rows = rw.shape[0]
    n_chunk = rows // (n_batch * chunk)
    rowspec = lambda w: pl.BlockSpec((chunk, w), lambda b, c: (b * n_chunk + c, 0))
    const = lambda a: pl.BlockSpec(a.shape, lambda b, c: (0,) * a.ndim)
    state = pl.BlockSpec((1, RW_HEADS, RW_N, RW_N), lambda b, c: (b, 0, 0, 0))
    params = [p['rw_mu'], p['rw_w0'], p['rw_w2'], p['rw_a0'], p['rw_a2'], p['rw_g2'], p['rw_kk'], p['rw_ka'],
              p['rw_rk'], p['rw_ln_w'], p['rw_ln_b']]
    return pl.pallas_call(
        functools.partial(_rwkv_kernel, chunk=chunk, t_valid=t_valid),
        grid=(n_batch, n_chunk),
        in_specs=[rowspec(SHIFT_W), rowspec(SHIFT_W), state] + [const(a) for a in params],
        out_specs=[rowspec(RW_W), state],
        out_shape=[jax.ShapeDtypeStruct((rows, RW_W), BF16),
                   jax.ShapeDtypeStruct((n_batch, RW_HEADS, RW_N, RW_N), F32)],
        compiler_params=pltpu.CompilerParams(dimension_semantics=("arbitrary", "arbitrary"),
                                             vmem_limit_bytes=VMEM_LIMIT),
        name="rwkv7",
    )(rw, prev, s0, *params)


def _rms(x, g):
    return x * lax.rsqrt(jnp.mean(x * x, axis=-1, keepdims=True) + NORM_EPS) * g


def _merge_ffn_kernel(x_ref, ya_ref, yr_ref, gate_ref, wa_ref, wr_ref, wo_ref, wu_ref, wd_ref,
                      n1_ref, n2pre_ref, n2_ref, out_ref):
    d = functools.partial(jnp.dot, preferred_element_type=F32)
    merged = gate_ref[:, :D_MODEL] * d(ya_ref[...], wa_ref[...]) + gate_ref[:, D_MODEL:] * d(yr_ref[...], wr_ref[...])
    x1 = x_ref[...] + _rms(d(merged.astype(BF16), wo_ref[...]), n1_ref[...])
    h2 = _rms(x1, n2pre_ref[...]).astype(BF16)
    u = jnp.square(jnp.maximum(d(h2, wu_ref[...]), 0.0))
    out_ref[...] = x1 + _rms(d(u.astype(BF16), wd_ref[...]), n2_ref[...])


def _merge_ffn(x, y_att, y_rw, gates, wa, wr, wo, wu, wd, n1_post, n2_pre, n2_post):
    rows = x.shape[0]
    tm = 256
    row = lambda w: pl.BlockSpec((tm, w), lambda i: (i, 0))
    const = lambda a: pl.BlockSpec(a.shape, lambda i: (0, 0), pipeline_mode=pl.Buffered(1))
    consts = [wa, wr, wo, wu, wd, n1_post, n2_pre, n2_post]
    return pl.pallas_call(
        _merge_ffn_kernel,
        grid=(rows // tm,),
        in_specs=[row(D_MODEL), row(ATT_W), row(RW_W), row(GATE_W)] + [const(a) for a in consts],
        out_specs=row(D_MODEL),
        out_shape=jax.ShapeDtypeStruct((rows, D_MODEL), F32),
        compiler_params=pltpu.CompilerParams(dimension_semantics=("arbitrary",), vmem_limit_bytes=VMEM_LIMIT),
        name="merge_ffn",
    )(x, y_att, y_rw, gates, *consts)


PAGES_PER_BLOCK = MOBA_BLOCK // PAGE_SIZE
WAVE = 8
SEL_BITS = 5


def _sample_select_kernel(pt_ref, qt_ref, ck_ref, sel_ref, buf, sem, sc_ref, *, n_pages, n_batch, dec_seq):
    b = pl.program_id(0)
    waves_per_b = n_pages // WAVE
    assert waves_per_b % 2 == 0
    n_full = n_pages // PAGES_PER_BLOCK

    def copies(bb, w, slot):
        return [pltpu.make_async_copy(ck_ref.at[pt_ref[bb, w * WAVE + i]], buf.at[slot, i], sem.at[slot])
                for i in range(WAVE)]

    @pl.when(b == 0)
    def _():
        for cp in copies(0, 0, 0):
            cp.start()

    qt = qt_ref[0]
    lane = lax.broadcasted_iota(jnp.int32, (HEAD_DIM, LANES), 1)
    for w in range(waves_per_b):
        slot = w % 2
        if w + 1 < waves_per_b:
            for cp in copies(b, w + 1, 1 - slot):
                cp.start()
        else:
            @pl.when(b + 1 < n_batch)
            def _():
                for cp in copies(b + 1, 0, 1 - slot):
                    cp.start()
        for cp in copies(b, w, slot):
            cp.wait()
        for i in range(0, WAVE, PAGES_PER_BLOCK):
            msel = jnp.zeros((HEAD_DIM, LANES), F32)
            for hh in range(N_HEADS):
                t = buf[slot, i, hh]
                for j in range(1, PAGES_PER_BLOCK):
                    t = t + buf[slot, i + j, hh]
                ksum = jnp.sum(t, axis=1, keepdims=True)
                msel = jnp.where(lane // dec_seq == hh, ksum, msel)
            n = (w * WAVE + i) // PAGES_PER_BLOCK
            sc_ref[n:n + 1, :] = jnp.sum(msel * qt, axis=0, keepdims=True) * (1.0 / MOBA_BLOCK)

    sc = sc_ref[...]
    blk = lax.broadcasted_iota(jnp.int32, (n_full, LANES), 0)
    row = lax.broadcasted_iota(jnp.int32, (8, LANES), 0)
    out = jnp.zeros((8, LANES), jnp.int32)
    for r in range(MOBA_TOPK):
        m = jnp.max(sc, axis=0, keepdims=True)
        idx = jnp.min(jnp.where(sc == m, blk, n_full), axis=0, keepdims=True)
        out = jnp.where(row == r, idx, out)
        sc = jnp.where(blk == idx, -jnp.inf, sc)
    sel_ref[0] = out


def _sample_select(page_table, q_t, ck_view, dec_seq):
    n_batch, n_pages = page_table.shape
    n_full = n_pages // PAGES_PER_BLOCK
    return pl.pallas_call(
        functools.partial(_sample_select_kernel, n_pages=n_pages, n_batch=n_batch, dec_seq=dec_seq),
        grid_spec=pltpu.PrefetchScalarGridSpec(
            num_scalar_prefetch=1,
            grid=(n_batch,),
            in_specs=[pl.BlockSpec((1, HEAD_DIM, LANES), lambda b, pt: (b, 0, 0)),
                      pl.BlockSpec(memory_space=pl.ANY)],
            out_specs=pl.BlockSpec((1, 8, LANES), lambda b, pt: (b, 0, 0)),
            scratch_shapes=[pltpu.VMEM((2, WAVE, N_HEADS, HEAD_DIM, PAGE_SIZE), F32),
                            pltpu.SemaphoreType.DMA((2,)),
                            pltpu.VMEM((n_full, LANES), F32)],
        ),
        out_shape=jax.ShapeDtypeStruct((n_batch, 8, LANES), jnp.int32),
        compiler_params=pltpu.CompilerParams(dimension_semantics=("arbitrary",), vmem_limit_bytes=VMEM_LIMIT),
        name="sample_select",
    )(page_table, q_t, ck_view)


def _sample_attend_kernel(pt_ref, sel_ref, wq_ref, kn_ref, vn_ref, ck_ref, cv_ref, o_ref, kbuf, vbuf, sem,
                          *, n_batch, dec_seq):
    b = pl.program_id(0)
    h = pl.program_id(1)
    seg = MOBA_TOPK * MOBA_BLOCK

    def copies(bb, hh, slot):
        cps = []
        for qi in range(dec_seq):
            packed = sel_ref[bb, hh * dec_seq + qi]
            for r in range(MOBA_TOPK):
                blk = (packed >> (SEL_BITS * r)) & ((1 << SEL_BITS) - 1)
                for pg in range(PAGES_PER_BLOCK):
                    page = pt_ref[bb, blk * PAGES_PER_BLOCK + pg]
                    t = (qi * MOBA_TOPK + r) * PAGES_PER_BLOCK + pg
                    dst = pl.ds(t * PAGE_SIZE, PAGE_SIZE)
                    cps.append(pltpu.make_async_copy(ck_ref.at[page, hh], kbuf.at[slot, :, dst], sem.at[slot, 0]))
                    cps.append(pltpu.make_async_copy(cv_ref.at[page, hh], vbuf.at[slot, :, dst], sem.at[slot, 1]))
        return cps

    slot = h % 2

    @pl.when(jnp.logical_and(b == 0, h == 0))
    def _():
        for cp in copies(0, 0, 0):
            cp.start()

    @pl.when(h + 1 < N_HEADS)
    def _():
        for cp in copies(b, h + 1, 1 - slot):
            cp.start()

    @pl.when(jnp.logical_and(h + 1 == N_HEADS, b + 1 < n_batch))
    def _():
        for cp in copies(b + 1, 0, 1 - slot):
            cp.start()

    for cp in copies(b, h, slot):
        cp.wait()

    wq = wq_ref[0, 0].astype(BF16)
    rows = wq.shape[0]
    s = jnp.dot(wq, kbuf[slot].astype(BF16), preferred_element_type=F32)
    col = lax.broadcasted_iota(jnp.int32, s.shape, 1)
    rr = lax.broadcasted_iota(jnp.int32, s.shape, 0)
    s = jnp.where(col // seg == rr, s, NEG_INF)
    s_own = jnp.dot(wq, kn_ref[0, 0].astype(BF16), preferred_element_type=F32)
    lane = lax.broadcasted_iota(jnp.int32, (rows, LANES), 1)
    r2 = lax.broadcasted_iota(jnp.int32, (rows, LANES), 0)
    s_own = jnp.where(jnp.logical_and(lane <= r2, r2 < dec_seq), s_own, NEG_INF)
    m = jnp.maximum(jnp.max(s, axis=1, keepdims=True), jnp.max(s_own, axis=1, keepdims=True))
    p = jnp.exp(s - m)
    p_own = jnp.exp(s_own - m)
    l = jnp.sum(p, axis=1, keepdims=True) + jnp.sum(p_own, axis=1, keepdims=True)
    o = lax.dot_general(p.astype(BF16), vbuf[slot].astype(BF16), NT_DIMS, preferred_element_type=F32)
    o = o + lax.dot_general(p_own.astype(BF16), vn_ref[0, 0].astype(BF16), NT_DIMS, preferred_element_type=F32)
    o_ref[0, 0] = o * (1.0 / l)


def _sample_attend(page_table, sel_packed, wq, kn_t, vn_t, ck_view, cv_view, dec_seq):
    n_batch = page_table.shape[0]
    rows = wq.shape[2]
    n_keys = dec_seq * MOBA_TOPK * MOBA_BLOCK
    qblk = pl.BlockSpec((1, 1, rows, HEAD_DIM), lambda b, h, pt, sel: (b, h, 0, 0))
    nblk = pl.BlockSpec((1, 1, HEAD_DIM, LANES), lambda b, h, pt, sel: (b, h, 0, 0))
    return pl.pallas_call(
        functools.partial(_sample_attend_kernel, n_batch=n_batch, dec_seq=dec_seq),
        grid_spec=pltpu.PrefetchScalarGridSpec(
            num_scalar_prefetch=2,
            grid=(n_batch, N_HEADS),
            in_specs=[qblk, nblk, nblk, pl.BlockSpec(memory_space=pl.ANY), pl.BlockSpec(memory_space=pl.ANY)],
            out_specs=qblk,
            scratch_shapes=[pltpu.VMEM((2, HEAD_DIM, n_keys), F32),
                            pltpu.VMEM((2, HEAD_DIM, n_keys), F32),
                            pltpu.SemaphoreType.DMA((2, 2))],
        ),
        out_shape=jax.ShapeDtypeStruct((n_batch, N_HEADS, rows, HEAD_DIM), F32),
        compiler_params=pltpu.CompilerParams(dimension_semantics=("arbitrary", "arbitrary"),
                                             vmem_limit_bytes=VMEM_LIMIT),
        name="sample_attend",
    )(page_table, sel_packed, wq, kn_t, vn_t, ck_view, cv_view)


def kernel(x_prompt, x_sample, cache_k, cache_v, state_wkv, state_shift, page_table, n1_pre, n1_post, n2_pre, n2_post, w_in, rw_mu, rw_w0, rw_w2, rw_a0, rw_a2, rw_g2, rw_kk, rw_ka, rw_rk, rw_ln_w, rw_ln_b, w_att_br, w_rw_br, w_out, w_up, w_down):
    _, seq, _ = x_prompt.shape
    n_batch, dec_seq, _ = x_sample.shape
    n_pages = page_table.shape[1]
    past = n_pages * PAGE_SIZE
    assert past % MOBA_BLOCK == 0 and n_pages // PAGES_PER_BLOCK >= MOBA_TOPK
    assert n_pages // PAGES_PER_BLOCK <= (1 << SEL_BITS) and x_prompt.shape[0] == 1 and n1_pre.shape[0] == 1
    l = 0
    row2 = lambda a: a[l].reshape(1, -1)
    rwp = dict(rw_mu=row2(rw_mu), rw_w0=row2(rw_w0), rw_w2=rw_w2[l], rw_a0=row2(rw_a0), rw_a2=rw_a2[l],
               rw_g2=rw_g2[l], rw_kk=row2(rw_kk), rw_ka=row2(rw_ka), rw_rk=row2(rw_rk),
               rw_ln_w=row2(rw_ln_w), rw_ln_b=row2(rw_ln_b))
    w_in_b = w_in[l].astype(BF16)
    ffn_w = [w.astype(BF16) for w in (w_att_br[l], w_rw_br[l], w_out[l], w_up[l], w_down[l])]
    norms = [row2(n1_post), row2(n2_pre), row2(n2_post)]
    heads_t = lambda t: t.reshape(N_HEADS, HEAD_DIM, -1)

    xp = x_prompt[0]
    cos, sin = _rope_tables(jnp.arange(seq))
    q, kb, vtb, kt, vt, rw, gate, kmean = _project(xp, cos, sin, row2(n1_pre), w_in_b)
    n_blk = seq // MOBA_BLOCK
    y_att = _moba_prompt(q, kb.reshape(n_blk, MOBA_BLOCK, ATT_W), vtb, kmean.reshape(n_blk, ATT_W))
    prev = jnp.concatenate([jnp.zeros((1, SHIFT_W), F32), rw[:-1]], axis=0)
    y_rw, wkv_p = _rwkv(rw, prev, jnp.zeros((1, RW_HEADS, RW_N, RW_N), F32), rwp, n_batch=1, chunk=64, t_valid=64)
    y_prompt = _merge_ffn(xp, y_att, y_rw, gate, *ffn_w, *norms)
    k_prompt = heads_t(kt).transpose(0, 2, 1)[None, None]
    v_prompt = heads_t(vt).transpose(0, 2, 1)[None, None]

    xs = x_sample.reshape(n_batch * dec_seq, D_MODEL)
    cos_s, sin_s = _rope_tables(jnp.tile(past + jnp.arange(dec_seq), n_batch))
    qs, _, _, kt_s, vt_s, rw_s, gate_s, _ = _project(xs, cos_s, sin_s, row2(n1_pre), w_in_b)
    kn_t = heads_t(kt_s).reshape(N_HEADS, HEAD_DIM, n_batch, dec_seq)
    vn_t = heads_t(vt_s).reshape(N_HEADS, HEAD_DIM, n_batch, dec_seq)
    q4 = qs.reshape(n_batch, dec_seq, N_HEADS, HEAD_DIM)
    q_t = q4.transpose(0, 3, 2, 1).reshape(n_batch, HEAD_DIM, N_HEADS * dec_seq)
    q_t = jnp.pad(q_t, ((0, 0), (0, 0), (0, LANES - N_HEADS * dec_seq)))
    ck_view = jnp.swapaxes(cache_k[:, l], -1, -2)
    cv_view = jnp.swapaxes(cache_v[:, l], -1, -2)
    sel = _sample_select(page_table, q_t, ck_view, dec_seq)
    sel = sel[:, :MOBA_TOPK, :N_HEADS * dec_seq]
    sel_packed = sel[:, 0] | (sel[:, 1] << SEL_BITS) | (sel[:, 2] << (2 * SEL_BITS))
    wq = jnp.pad(q4.transpose(0, 2, 1, 3) * (HEAD_DIM ** -0.5), ((0, 0), (0, 0), (0, 8 - dec_seq), (0, 0)))
    pad_new = lambda t: jnp.pad(t.transpose(2, 0, 1, 3), ((0, 0), (0, 0), (0, 0), (0, LANES - dec_seq)))
    o_s = _sample_attend(page_table, sel_packed, wq, pad_new(kn_t), pad_new(vn_t), ck_view, cv_view, dec_seq)
    y_att_s = o_s[:, :, :dec_seq].transpose(0, 2, 1, 3).reshape(n_batch * dec_seq, ATT_W).astype(BF16)

    chunk_s = 8
    rw_s3 = rw_s.reshape(n_batch, dec_seq, SHIFT_W)
    prev_s = jnp.concatenate([state_shift[:, l][:, None], rw_s3[:, :-1]], axis=1)
    pad_t = lambda t: jnp.pad(t, ((0, 0), (0, chunk_s - dec_seq), (0, 0))).reshape(n_batch * chunk_s, SHIFT_W)
    y_rw_s, wkv_s = _rwkv(pad_t(rw_s3), pad_t(prev_s), state_wkv[:, l], rwp, n_batch=n_batch, chunk=chunk_s,
                          t_valid=dec_seq)
    y_rw_s = y_rw_s.reshape(n_batch, chunk_s, RW_W)[:, :dec_seq].reshape(n_batch * dec_seq, RW_W)
    y_sample = _merge_ffn(xs, y_att_s, y_rw_s, gate_s, *ffn_w, *norms)
    k_sample = kn_t.transpose(2, 0, 3, 1)[:, None]
    v_sample = vn_t.transpose(2, 0, 3, 1)[:, None]

    return (y_prompt[None], y_sample.reshape(n_batch, dec_seq, D_MODEL), k_prompt, v_prompt, k_sample, v_sample,
            wkv_p[:, None], wkv_s[:, None], rw[-1][None, None], rw_s3[:, -1][:, None])
```

```python
import functools
import math

import jax
import jax.numpy as jnp
from jax import lax
from jax.experimental import pallas as pl
from jax.experimental.pallas import tpu as pltpu

F32 = jnp.float32
BF16 = jnp.bfloat16

D_MODEL = 1024
N_HEADS = 8
HEAD_DIM = 64
ATT_W = N_HEADS * HEAD_DIM
RW_HEADS = 8
RW_N = 64
RW_W = RW_HEADS * RW_N
DECAY_LORA = 64
AAA_LORA = 64
GATE_LORA = 128
D_FF = 4 * D_MODEL
MOBA_BLOCK = 256
MOBA_TOPK = 3
Q_BLOCK = 128
PAGE_SIZE = 128
ROPE_THETA = 10000.0
NORM_EPS = 1e-6
GN_EPS = 64e-5
NEG_INF = -1e30
LOG2_E = 1.4426950408889634
SHIFT_W = 3 * RW_W + DECAY_LORA + AAA_LORA + GATE_LORA
GATE_W = 2 * D_MODEL
IN_W = 3 * ATT_W + SHIFT_W + GATE_W

LANES = 128
ONES_ROWS = 16
MOBA_UNROLL = 8
MOBA_AHEAD = 3
VMEM_LIMIT = 56 * 1024 * 1024

NT_DIMS = (((1,), (1,)), ((), ()))
TN_DIMS = (((0,), (0,)), ((), ()))


def _split2(x):
    hi = x.astype(BF16)
    lo = (x - hi.astype(F32)).astype(BF16)
    return hi, lo


def _dot3(a, b, dims=(((1,), (0,)), ((), ()))):
    ah, al = _split2(a)
    bh, bl = _split2(b)
    d = functools.partial(lax.dot_general, dimension_numbers=dims, preferred_element_type=F32)
    return d(ah, bh) + (d(ah, bl) + d(al, bh))


def _split3(x):
    x1 = x.astype(BF16)
    r1 = x - x1.astype(F32)
    x2 = r1.astype(BF16)
    x3 = (r1 - x2.astype(F32)).astype(BF16)
    return x1, x2, x3


def _dot_exact_lhs(mask, b):
    a = mask.astype(BF16)
    b1, b2, b3 = _split3(b)
    d = functools.partial(jnp.dot, preferred_element_type=F32)
    return d(a, b1) + (d(a, b2) + d(a, b3))


def _dot_exact_rhs(a, mask):
    a1, a2, a3 = _split3(a)
    o = mask.astype(BF16)
    d = functools.partial(jnp.dot, preferred_element_type=F32)
    return d(a1, o) + (d(a2, o) + d(a3, o))


def _sigmoid(x):
    return 1.0 / (1.0 + jnp.exp(-x))


def _proj_kernel(x_ref, cos_ref, sin_ref, g_ref, w_ref,
                 q_ref, kb_ref, vtb_ref, kt_ref, vt_ref, rw_ref, gate_ref, kmean_ref, *, tm):
    x = x_ref[...]
    ms = jnp.mean(x * x, axis=-1, keepdims=True)
    h = (x * lax.rsqrt(ms + NORM_EPS) * g_ref[...]).astype(BF16)
    cos = cos_ref[...]
    sin = sin_ref[...]
    lane = lax.broadcasted_iota(jnp.int32, (tm, LANES), 1)
    first_half = (lane % HEAD_DIM) < (HEAD_DIM // 2)

    def rope(p):
        sw = jnp.where(first_half, pltpu.roll(p, LANES - HEAD_DIM // 2, 1), pltpu.roll(p, HEAD_DIM // 2, 1))
        return p * cos + sw * sin

    def proj(lo, width):
        return jnp.dot(h, w_ref[:, lo:lo + width], preferred_element_type=F32)

    pq = proj(0, ATT_W)
    for c in range(ATT_W // LANES):
        q_ref[:, c * LANES:(c + 1) * LANES] = rope(pq[:, c * LANES:(c + 1) * LANES])
    pk = proj(ATT_W, ATT_W)
    for c in range(ATT_W // LANES):
        cols = slice(c * LANES, (c + 1) * LANES)
        kr = rope(pk[:, cols])
        kb_ref[:, cols] = kr.astype(BF16)
        kt_ref[cols, :] = kr.T
        kmean_ref[0, :, cols] = jnp.sum(kr, axis=0, keepdims=True) * (1.0 / tm)
    pv = proj(2 * ATT_W, ATT_W)
    for c in range(ATT_W // LANES):
        cols = slice(c * LANES, (c + 1) * LANES)
        vt = pv[:, cols].T
        vt_ref[cols, :] = vt
        vtb_ref[0, cols, :] = vt.astype(BF16)
    rw_ref[...] = proj(3 * ATT_W, SHIFT_W)
    gate_ref[...] = _sigmoid(proj(3 * ATT_W + SHIFT_W, GATE_W))


def _rope_tables(pos):
    half = HEAD_DIM // 2
    inv = ROPE_THETA ** (-jnp.arange(half, dtype=F32) / half)
    ang = pos.astype(F32)[:, None] * inv[None, :]
    cos = jnp.cos(ang)
    sin = jnp.sin(ang)
    return jnp.tile(jnp.concatenate([cos, cos], axis=1), (1, 2)), jnp.tile(jnp.concatenate([-sin, sin], axis=1), (1, 2))


def _project(x, cos, sin, g, w_in_bf16):
    rows = x.shape[0]
    tm = MOBA_BLOCK
    n = rows // tm
    row = lambda w: pl.BlockSpec((tm, w), lambda i: (i, 0))
    const = lambda a, b: pl.BlockSpec((a, b), lambda i: (0, 0))
    col = pl.BlockSpec((ATT_W, tm), lambda i: (0, i))
    return pl.pallas_call(
        functools.partial(_proj_kernel, tm=tm),
        grid=(n,),
        in_specs=[row(D_MODEL), row(LANES), row(LANES), const(1, D_MODEL), const(D_MODEL, IN_W)],
        out_specs=[row(ATT_W), row(ATT_W), pl.BlockSpec((1, ATT_W, tm), lambda i: (i, 0, 0)), col, col,
                   row(SHIFT_W), row(GATE_W), pl.BlockSpec((1, 1, ATT_W), lambda i: (i, 0, 0))],
        out_shape=[
            jax.ShapeDtypeStruct((rows, ATT_W), F32),
            jax.ShapeDtypeStruct((rows, ATT_W), BF16),
            jax.ShapeDtypeStruct((n, ATT_W, tm), BF16),
            jax.ShapeDtypeStruct((ATT_W, rows), F32),
            jax.ShapeDtypeStruct((ATT_W, rows), F32),
            jax.ShapeDtypeStruct((rows, SHIFT_W), F32),
            jax.ShapeDtypeStruct((rows, GATE_W), F32),
            jax.ShapeDtypeStruct((n, 1, ATT_W), F32),
        ],
        compiler_params=pltpu.CompilerParams(dimension_semantics=("arbitrary",), vmem_limit_bytes=VMEM_LIMIT),
        name="project",
    )(x, cos, sin, g, w_in_bf16)


def _moba_prompt_kernel(q_ref, k_ref, vt_ref, kmean_ref, y_ref, sel_ref, *, n_blk):
    c = pl.program_id(1)
    ob = (c * Q_BLOCK) // MOBA_BLOCK
    q = q_ref[...]
    lane = lax.broadcasted_iota(jnp.int32, (Q_BLOCK, LANES), 1)
    kmean = kmean_ref[...]
    blk = lax.broadcasted_iota(jnp.int32, (n_blk, Q_BLOCK), 0)
    qs = []
    for hh in range(2):
        qh = jnp.where((lane // HEAD_DIM) == hh, q, 0.0)
        qs.append((qh * (HEAD_DIM ** -0.5 * LOG2_E)).astype(BF16))
        sb = _dot3(kmean, qh, NT_DIMS)
        sb = jnp.where(blk < ob, sb, NEG_INF)
        sel = jnp.zeros((n_blk, Q_BLOCK), jnp.bool_)
        for _ in range(MOBA_TOPK):
            m = jnp.max(sb, axis=0, keepdims=True)
            idx = jnp.min(jnp.where(sb == m, blk, n_blk), axis=0, keepdims=True)
            pick = blk == idx
            sel = jnp.logical_or(sel, pick)
            sb = jnp.where(pick, -jnp.inf, sb)
        sel = jnp.logical_and(sel, blk < ob)
        sel_ref[:, hh * Q_BLOCK:(hh + 1) * Q_BLOCK] = sel.astype(F32)
    qs = jnp.concatenate(qs, axis=0)

    def scores(j):
        return lax.dot_general(k_ref[j], qs, NT_DIMS, preferred_element_type=F32)

    s = scores(ob)
    key = lax.broadcasted_iota(jnp.int32, (MOBA_BLOCK, 2 * Q_BLOCK), 0)
    qi = lax.broadcasted_iota(jnp.int32, (MOBA_BLOCK, 2 * Q_BLOCK), 1) % Q_BLOCK
    s = jnp.where(key <= (c % 2) * Q_BLOCK + qi, s, NEG_INF)
    m0 = jnp.max(s, axis=0, keepdims=True)
    ones_rows = jnp.ones((ONES_ROWS, MOBA_BLOCK), BF16)

    def pv(j, p):
        return jnp.dot(jnp.concatenate([vt_ref[j], ones_rows], axis=0), p.astype(BF16), preferred_element_type=F32)

    acc0 = pv(ob, jnp.exp2(s - m0))

    def body(i, carry):
        m, acc, queue = carry
        queue = list(queue)
        for t in range(MOBA_UNROLL):
            j = jnp.minimum(MOBA_UNROLL * i + t, n_blk - 1)
            queue.append(scores(jnp.minimum(MOBA_UNROLL * i + t + MOBA_AHEAD, n_blk - 1)))
            s = jnp.where(sel_ref[pl.ds(j, 1), :] > 0.0, queue.pop(0), NEG_INF)
            m_new = jnp.maximum(m, jnp.max(s, axis=0, keepdims=True))
            acc = jnp.exp2(m - m_new) * acc + pv(j, jnp.exp2(s - m_new))
            m = m_new
        return m, acc, tuple(queue)

    ahead = tuple(scores(jnp.minimum(t, n_blk - 1)) for t in range(MOBA_AHEAD))
    _, acc, _ = lax.fori_loop(0, (ob + MOBA_UNROLL - 1) // MOBA_UNROLL, body, (m0, acc0, ahead))
    o = acc[:LANES] * (1.0 / acc[LANES:LANES + 1])
    row = lax.broadcasted_iota(jnp.int32, (LANES, Q_BLOCK), 0)
    ot = jnp.where(row < HEAD_DIM, o[:, :Q_BLOCK], o[:, Q_BLOCK:])
    y_ref[...] = ot.T.astype(y_ref.dtype)


def _moba_prompt(q, k_blocks, vt_blocks, kmean):
    s = q.shape[0]
    n_blk = k_blocks.shape[0]
    n_chunk = s // Q_BLOCK
    return pl.pallas_call(
        functools.partial(_moba_prompt_kernel, n_blk=n_blk),
        grid=(ATT_W // LANES, n_chunk),
        in_specs=[
            pl.BlockSpec((Q_BLOCK, LANES), lambda hp, c: (c, hp)),
            pl.BlockSpec((n_blk, MOBA_BLOCK, LANES), lambda hp, c: (0, 0, hp)),
            pl.BlockSpec((n_blk, LANES, MOBA_BLOCK), lambda hp, c: (0, hp, 0)),
            pl.BlockSpec((n_blk, LANES), lambda hp, c: (0, hp)),
        ],
        out_specs=pl.BlockSpec((Q_BLOCK, LANES), lambda hp, c: (c, hp)),
        out_shape=jax.ShapeDtypeStruct((s, ATT_W), BF16),
        scratch_shapes=[pltpu.VMEM((n_blk, 2 * Q_BLOCK), F32)],
        compiler_params=pltpu.CompilerParams(dimension_semantics=("arbitrary", "arbitrary"),
                                             vmem_limit_bytes=VMEM_LIMIT),
        name="moba_prompt",
    )(q, k_blocks, vt_blocks, kmean)


def _head_ones():
    li = lax.broadcasted_iota(jnp.int32, (RW_W, RW_W), 0) // RW_N
    lj = lax.broadcasted_iota(jnp.int32, (RW_W, RW_W), 1) // RW_N
    return (li == lj).astype(F32)


def _rwkv_mix(rw, prev, mu_ref, w0_ref, w2_ref, a0_ref, a2_ref, g2_ref, kk_ref, ka_ref):
    m = rw + (prev - rw) * mu_ref[...]
    r = m[:, 0:RW_W]
    k = m[:, RW_W:2 * RW_W]
    v = m[:, 2 * RW_W:3 * RW_W]
    o1 = 3 * RW_W
    wd = m[:, o1:o1 + DECAY_LORA]
    ad = m[:, o1 + DECAY_LORA:o1 + DECAY_LORA + AAA_LORA]
    gd = m[:, o1 + DECAY_LORA + AAA_LORA:]
    bdot = lambda a, b: jnp.dot(a.astype(BF16), b.astype(BF16), preferred_element_type=F32)
    w_raw = w0_ref[...] + bdot(jnp.tanh(wd), w2_ref[...])
    logw = -math.exp(-0.5) * _sigmoid(w_raw)
    a_sig = _sigmoid(a0_ref[...] + bdot(ad, a2_ref[...]))
    g = bdot(_sigmoid(gd), g2_ref[...])

    kk = k * kk_ref[...]
    kk_norm = jnp.sqrt(_dot_exact_rhs(kk * kk, _head_ones()))
    kk = kk / jnp.maximum(kk_norm, 1e-12)
    k2 = k * (1.0 + (a_sig - 1.0) * ka_ref[...])
    return r, logw, k2, v, -kk, kk * a_sig, g


def _rwkv_kernel(rw_ref, prev_ref, mu_ref, w0_ref, w2_ref, a0_ref, a2_ref, g2_ref, kk_ref, ka_ref,
                 rk_ref, lnw_ref, lnb_ref, y_ref, s_ref, *, chunk):
    @pl.when(pl.program_id(0) == 0)
    def _():
        s_ref[...] = jnp.zeros_like(s_ref)

    r, logw, k2, v, a_vec, b_vec, g = _rwkv_mix(rw_ref[...], prev_ref[...], mu_ref, w0_ref, w2_ref, a0_ref, a2_ref,
                                                g2_ref, kk_ref, ka_ref)
    head_ones = _head_ones()
    ti = lax.broadcasted_iota(jnp.int32, (chunk, chunk), 0)
    tj = lax.broadcasted_iota(jnp.int32, (chunk, chunk), 1)
    lower_incl = tj <= ti
    lower_strict = tj < ti
    cs = _dot_exact_lhs(lower_incl.astype(F32), logw)
    p_incl = jnp.exp(cs)
    p_inv = jnp.exp(-cs)
    a_t = a_vec * jnp.exp(cs - logw)
    r_t = r * p_incl
    b_t = b_vec * p_inv
    k_t = k2 * p_inv
    p_last = p_incl[chunk - 1:chunk, :]
    eye = (ti == tj).astype(F32)

    heads = range(RW_HEADS)
    sl = [slice(h * RW_N, (h + 1) * RW_N) for h in heads]
    ar = [jnp.concatenate([a_t[:, s], r_t[:, s]], axis=0) for s in sl]
    bk = [jnp.concatenate([b_t[:, s], k_t[:, s]], axis=0) for s in sl]
    gram = [_dot3(ar[h], bk[h], NT_DIMS) for h in heads]
    s0 = [s_ref[h] for h in heads]
    ars = [_dot3(ar[h], s0[h], NT_DIMS) for h in heads]
    l_ak_v = [_dot3(jnp.where(lower_strict, gram[h][:chunk, chunk:], 0.0), v[:, sl[h]]) for h in heads]
    l_ab = [jnp.where(lower_strict, gram[h][:chunk, :chunk], 0.0) for h in heads]
    x = [eye + l_ab[h] for h in heads]
    pw = [_dot3(l_ab[h], l_ab[h]) for h in heads]
    n_factors = int(math.log2(chunk))
    for it in range(1, n_factors):
        if it + 1 < n_factors:
            prod = [_dot3(pw[h], jnp.concatenate([pw[h], x[h]], axis=1)) for h in heads]
            pw = [prod[h][:, :chunk] for h in heads]
            x = [x[h] + prod[h][:, chunk:] for h in heads]
        else:
            x = [x[h] + _dot3(pw[h], x[h]) for h in heads]
    u = [_dot3(x[h], ars[h][:chunk] + l_ak_v[h]) for h in heads]
    uv = [jnp.concatenate([u[h], v[:, sl[h]]], axis=0) for h in heads]
    wide_i = lax.broadcasted_iota(jnp.int32, (chunk, 2 * chunk), 0)
    wide_j = lax.broadcasted_iota(jnp.int32, (chunk, 2 * chunk), 1) % chunk
    m_r = [jnp.where(wide_j <= wide_i, gram[h][chunk:, :], 0.0) for h in heads]
    ys = [ars[h][chunk:] + _dot3(m_r[h], uv[h]) for h in heads]
    for h in heads:
        s_ref[h] = (s0[h] + _dot3(uv[h], bk[h], TN_DIMS)) * p_last[:, sl[h]]
    o = jnp.concatenate(ys, axis=1)

    mean = _dot_exact_rhs(o, head_ones) * (1.0 / RW_N)
    d = o - mean
    var = _dot_exact_rhs(d * d, head_ones) * (1.0 / RW_N)
    o = d * lax.rsqrt(var + GN_EPS) * lnw_ref[...] + lnb_ref[...]
    bonus = _dot_exact_rhs(r * k2 * rk_ref[...], head_ones) * v
    y_ref[...] = ((o + bonus) * g).astype(y_ref.dtype)


RW_CHUNK = 64

_MIX_PARAMS = ('rw_mu', 'rw_w0', 'rw_w2', 'rw_a0', 'rw_a2', 'rw_g2', 'rw_kk', 'rw_ka')


def _rwkv_prompt(rw, prev, p):
    rows = rw.shape[0]
    rowspec = lambda w: pl.BlockSpec((RW_CHUNK, w), lambda c: (c, 0))
    const = lambda a: pl.BlockSpec(a.shape, lambda c: (0,) * a.ndim)
    params = [p[n] for n in _MIX_PARAMS] + [p['rw_rk'], p['rw_ln_w'], p['rw_ln_b']]
    return pl.pallas_call(
        functools.partial(_rwkv_kernel, chunk=RW_CHUNK),
        grid=(rows // RW_CHUNK,),
        in_specs=[rowspec(SHIFT_W), rowspec(SHIFT_W)] + [const(a) for a in params],
        out_specs=[rowspec(RW_W), pl.BlockSpec((RW_HEADS, RW_N, RW_N), lambda c: (0, 0, 0))],
        out_shape=[jax.ShapeDtypeStruct((rows, RW_W), BF16),
                   jax.ShapeDtypeStruct((RW_HEADS, RW_N, RW_N), F32)],
        compiler_params=pltpu.CompilerParams(dimension_semantics=("arbitrary",), vmem_limit_bytes=VMEM_LIMIT),
        name="rwkv7_prompt",
    )(rw, prev, *params)


def _rwkv_mix_kernel(rw_ref, prev_ref, mu_ref, w0_ref, w2_ref, a0_ref, a2_ref, g2_ref, kk_ref, ka_ref,
                     r_ref, w_ref, k_ref, v_ref, a_ref, b_ref, g_ref):
    r, logw, k2, v, a_vec, b_vec, g = _rwkv_mix(rw_ref[...], prev_ref[...], mu_ref, w0_ref, w2_ref, a0_ref, a2_ref,
                                                g2_ref, kk_ref, ka_ref)
    r_ref[...] = r
    w_ref[...] = jnp.exp(logw)
    k_ref[...] = k2
    v_ref[...] = v
    a_ref[...] = a_vec
    b_ref[...] = b_vec
    g_ref[...] = g


def _rwkv_mix_call(rw, prev, p):
    rows = rw.shape[0]
    rowspec = lambda w: pl.BlockSpec((rows, w), lambda i: (0, 0))
    const = lambda a: pl.BlockSpec(a.shape, lambda i: (0,) * a.ndim)
    params = [p[n] for n in _MIX_PARAMS]
    return pl.pallas_call(
        _rwkv_mix_kernel,
        grid=(1,),
        in_specs=[rowspec(SHIFT_W), rowspec(SHIFT_W)] + [const(a) for a in params],
        out_specs=[rowspec(RW_W)] * 7,
        out_shape=[jax.ShapeDtypeStruct((rows, RW_W), F32)] * 7,
        compiler_params=pltpu.CompilerParams(dimension_semantics=("arbitrary",), vmem_limit_bytes=VMEM_LIMIT),
        name="rwkv7_mix",
    )(rw, prev, *params)


def _rwkv_steps_kernel(r_ref, w_ref, k_ref, v_ref, a_ref, b_ref, g_ref, s_ref, lnw_ref, lnb_ref, rk_ref,
                       y_ref, so_ref, o_scr, *, n_steps):
    def value_row(vi, carry):
        s = s_ref[0, vi]
        for t in range(n_steps):
            sa = jnp.sum(s * a_ref[t, 0], axis=0, keepdims=True)
            s = s * w_ref[t, 0] + sa * b_ref[t, 0] + v_ref[t, 0, pl.ds(vi, 1), :] * k_ref[t, 0]
            o_scr[t, pl.ds(vi, 1), :] = jnp.sum(s * r_ref[t, 0], axis=0, keepdims=True)
        so_ref[0, vi] = s
        return carry

    lax.fori_loop(0, RW_N, value_row, 0)
    for t in range(n_steps):
        o = o_scr[t]
        d = o - jnp.mean(o, axis=0, keepdims=True)
        var = jnp.mean(d * d, axis=0, keepdims=True)
        bonus = jnp.sum(r_ref[t, 0] * k_ref[t, 0] * rk_ref[0], axis=0, keepdims=True)
        o = d * lax.rsqrt(var + GN_EPS) * lnw_ref[0] + lnb_ref[0]
        y_ref[t, 0] = (o + bonus * v_ref[t, 0]) * g_ref[t, 0]


def _rwkv_steps(vecs, state, lnw, lnb, rk):
    n_steps, _, _, n_batch = vecs[0].shape
    vec = pl.BlockSpec((n_steps, 1, RW_N, n_batch), lambda h: (0, h, 0, 0))
    st = pl.BlockSpec((1, RW_N, RW_N, n_batch), lambda h: (h, 0, 0, 0))
    par = pl.BlockSpec((1, RW_N, n_batch), lambda h: (h, 0, 0))
    return pl.pallas_call(
        functools.partial(_rwkv_steps_kernel, n_steps=n_steps),
        grid=(RW_HEADS,),
        in_specs=[vec] * 7 + [st, par, par, par],
        out_specs=[vec, st],
        out_shape=[jax.ShapeDtypeStruct(vecs[0].shape, F32), jax.ShapeDtypeStruct(state.shape, F32)],
        scratch_shapes=[pltpu.VMEM((n_steps, RW_N, n_batch), F32)],
        compiler_params=pltpu.CompilerParams(dimension_semantics=("arbitrary",), vmem_limit_bytes=VMEM_LIMIT),
        name="rwkv7_steps",
    )(*vecs, state, lnw, lnb, rk)


def _rms(x, g):
    return x * lax.rsqrt(jnp.mean(x * x, axis=-1, keepdims=True) + NORM_EPS) * g


def _merge_ffn_kernel(x_ref, ya_ref, yr_ref, gate_ref, wa_ref, wr_ref, wo_ref, wu_ref, wd_ref,
                      n1_ref, n2pre_ref, n2_ref, out_ref):
    d = functools.partial(jnp.dot, preferred_element_type=F32)
    merged = gate_ref[:, :D_MODEL] * d(ya_ref[...], wa_ref[...]) + gate_ref[:, D_MODEL:] * d(yr_ref[...], wr_ref[...])
    x1 = x_ref[...] + _rms(d(merged.astype(BF16), wo_ref[...]), n1_ref[...])
    h2 = _rms(x1, n2pre_ref[...]).astype(BF16)
    u = jnp.square(jnp.maximum(d(h2, wu_ref[...]), 0.0))
    out_ref[...] = x1 + _rms(d(u.astype(BF16), wd_ref[...]), n2_ref[...])


def _merge_ffn(x, y_att, y_rw, gates, wa, wr, wo, wu, wd, n1_post, n2_pre, n2_post):
    rows = x.shape[0]
    tm = 256
    row = lambda w: pl.BlockSpec((tm, w), lambda i: (i, 0))
    const = lambda a: pl.BlockSpec(a.shape, lambda i: (0, 0), pipeline_mode=pl.Buffered(1))
    consts = [wa, wr, wo, wu, wd, n1_post, n2_pre, n2_post]
    return pl.pallas_call(
        _merge_ffn_kernel,
        grid=(rows // tm,),
        in_specs=[row(D_MODEL), row(ATT_W), row(RW_W), row(GATE_W)] + [const(a) for a in consts],
        out_specs=row(D_MODEL),
        out_shape=jax.ShapeDtypeStruct((rows, D_MODEL), F32),
        compiler_params=pltpu.CompilerParams(dimension_semantics=("arbitrary",), vmem_limit_bytes=VMEM_LIMIT),
        name="merge_ffn",
    )(x, y_att, y_rw, gates, *consts)


PAGES_PER_BLOCK = MOBA_BLOCK // PAGE_SIZE
WAVE = 8
SEL_BITS = 5


def _sample_select_kernel(pt_ref, qt_ref, ck_ref, sel_ref, buf, sem, sc_ref, *, n_pages, n_batch, dec_seq):
    b = pl.program_id(0)
    waves_per_b = n_pages // WAVE
    assert waves_per_b % 2 == 0
    n_full = n_pages // PAGES_PER_BLOCK

    def copies(bb, w, slot):
        return [pltpu.make_async_copy(ck_ref.at[pt_ref[bb, w * WAVE + i]], buf.at[slot, i], sem.at[slot])
                for i in range(WAVE)]

    @pl.when(b == 0)
    def _():
        for cp in copies(0, 0, 0):
            cp.start()

    qt = qt_ref[0]
    lane = lax.broadcasted_iota(jnp.int32, (HEAD_DIM, LANES), 1)
    for w in range(waves_per_b):
        slot = w % 2
        if w + 1 < waves_per_b:
            for cp in copies(b, w + 1, 1 - slot):
                cp.start()
        else:
            @pl.when(b + 1 < n_batch)
            def _():
                for cp in copies(b + 1, 0, 1 - slot):
                    cp.start()
        for cp in copies(b, w, slot):
            cp.wait()
        for i in range(0, WAVE, PAGES_PER_BLOCK):
            msel = jnp.zeros((HEAD_DIM, LANES), F32)
            for hh in range(N_HEADS):
                t = buf[slot, i, hh]
                for j in range(1, PAGES_PER_BLOCK):
                    t = t + buf[slot, i + j, hh]
                ksum = jnp.sum(t, axis=1, keepdims=True)
                msel = jnp.where(lane // dec_seq == hh, ksum, msel)
            n = (w * WAVE + i) // PAGES_PER_BLOCK
            sc_ref[n:n + 1, :] = jnp.sum(msel * qt, axis=0, keepdims=True) * (1.0 / MOBA_BLOCK)

    sc = sc_ref[...]
    blk = lax.broadcasted_iota(jnp.int32, (n_full, LANES), 0)
    row = lax.broadcasted_iota(jnp.int32, (8, LANES), 0)
    out = jnp.zeros((8, LANES), jnp.int32)
    for r in range(MOBA_TOPK):
        m = jnp.max(sc, axis=0, keepdims=True)
        idx = jnp.min(jnp.where(sc == m, blk, n_full), axis=0, keepdims=True)
        out = jnp.where(row == r, idx, out)
        sc = jnp.where(blk == idx, -jnp.inf, sc)
    sel_ref[0] = out


def _sample_select(page_table, q_t, ck_view, dec_seq):
    n_batch, n_pages = page_table.shape
    n_full = n_pages // PAGES_PER_BLOCK
    return pl.pallas_call(
        functools.partial(_sample_select_kernel, n_pages=n_pages, n_batch=n_batch, dec_seq=dec_seq),
        grid_spec=pltpu.PrefetchScalarGridSpec(
            num_scalar_prefetch=1,
            grid=(n_batch,),
            in_specs=[pl.BlockSpec((1, HEAD_DIM, LANES), lambda b, pt: (b, 0, 0)),
                      pl.BlockSpec(memory_space=pl.ANY)],
            out_specs=pl.BlockSpec((1, 8, LANES), lambda b, pt: (b, 0, 0)),
            scratch_shapes=[pltpu.VMEM((2, WAVE, N_HEADS, HEAD_DIM, PAGE_SIZE), F32),
                            pltpu.SemaphoreType.DMA((2,)),
                            pltpu.VMEM((n_full, LANES), F32)],
        ),
        out_shape=jax.ShapeDtypeStruct((n_batch, 8, LANES), jnp.int32),
        compiler_params=pltpu.CompilerParams(dimension_semantics=("arbitrary",), vmem_limit_bytes=VMEM_LIMIT),
        name="sample_select",
    )(page_table, q_t, ck_view)


def _sample_attend_kernel(pt_ref, sel_ref, wq_ref, kn_ref, vn_ref, ck_ref, cv_ref, o_ref, kbuf, vbuf, sem,
                          *, n_batch, dec_seq):
    b = pl.program_id(0)
    h = pl.program_id(1)
    seg = MOBA_TOPK * MOBA_BLOCK

    def copies(bb, hh, slot):
        cps = []
        for qi in range(dec_seq):
            packed = sel_ref[bb, hh * dec_seq + qi]
            for r in range(MOBA_TOPK):
                blk = (packed >> (SEL_BITS * r)) & ((1 << SEL_BITS) - 1)
                for pg in range(PAGES_PER_BLOCK):
                    page = pt_ref[bb, blk * PAGES_PER_BLOCK + pg]
                    t = (qi * MOBA_TOPK + r) * PAGES_PER_BLOCK + pg
                    dst = pl.ds(t * PAGE_SIZE, PAGE_SIZE)
                    cps.append(pltpu.make_async_copy(ck_ref.at[page, hh], kbuf.at[slot, :, dst], sem.at[slot, 0]))
                    cps.append(pltpu.make_async_copy(cv_ref.at[page, hh], vbuf.at[slot, :, dst], sem.at[slot, 1]))
        return cps

    slot = h % 2

    @pl.when(jnp.logical_and(b == 0, h == 0))
    def _():
        for cp in copies(0, 0, 0):
            cp.start()

    @pl.when(h + 1 < N_HEADS)
    def _():
        for cp in copies(b, h + 1, 1 - slot):
            cp.start()

    @pl.when(jnp.logical_and(h + 1 == N_HEADS, b + 1 < n_batch))
    def _():
        for cp in copies(b + 1, 0, 1 - slot):
            cp.start()

    for cp in copies(b, h, slot):
        cp.wait()

    wq = wq_ref[0, 0].astype(BF16)
    rows = wq.shape[0]
    s = jnp.dot(wq, kbuf[slot].astype(BF16), preferred_element_type=F32)
    col = lax.broadcasted_iota(jnp.int32, s.shape, 1)
    rr = lax.broadcasted_iota(jnp.int32, s.shape, 0)
    s = jnp.where(col // seg == rr, s, NEG_INF)
    s_own = jnp.dot(wq, kn_ref[0, 0].astype(BF16), preferred_element_type=F32)
    lane = lax.broadcasted_iota(jnp.int32, (rows, LANES), 1)
    r2 = lax.broadcasted_iota(jnp.int32, (rows, LANES), 0)
    s_own = jnp.where(jnp.logical_and(lane <= r2, r2 < dec_seq), s_own, NEG_INF)
    m = jnp.maximum(jnp.max(s, axis=1, keepdims=True), jnp.max(s_own, axis=1, keepdims=True))
    p = jnp.exp(s - m)
    p_own = jnp.exp(s_own - m)
    l = jnp.sum(p, axis=1, keepdims=True) + jnp.sum(p_own, axis=1, keepdims=True)
    o = lax.dot_general(p.astype(BF16), vbuf[slot].astype(BF16), NT_DIMS, preferred_element_type=F32)
    o = o + lax.dot_general(p_own.astype(BF16), vn_ref[0, 0].astype(BF16), NT_DIMS, preferred_element_type=F32)
    o_ref[0, 0] = o * (1.0 / l)


def _sample_attend(page_table, sel_packed, wq, kn_t, vn_t, ck_view, cv_view, dec_seq):
    n_batch = page_table.shape[0]
    rows = wq.shape[2]
    n_keys = dec_seq * MOBA_TOPK * MOBA_BLOCK
    qblk = pl.BlockSpec((1, 1, rows, HEAD_DIM), lambda b, h, pt, sel: (b, h, 0, 0))
    nblk = pl.BlockSpec((1, 1, HEAD_DIM, LANES), lambda b, h, pt, sel: (b, h, 0, 0))
    return pl.pallas_call(
        functools.partial(_sample_attend_kernel, n_batch=n_batch, dec_seq=dec_seq),
        grid_spec=pltpu.PrefetchScalarGridSpec(
            num_scalar_prefetch=2,
            grid=(n_batch, N_HEADS),
            in_specs=[qblk, nblk, nblk, pl.BlockSpec(memory_space=pl.ANY), pl.BlockSpec(memory_space=pl.ANY)],
            out_specs=qblk,
            scratch_shapes=[pltpu.VMEM((2, HEAD_DIM, n_keys), F32),
                            pltpu.VMEM((2, HEAD_DIM, n_keys), F32),
                            pltpu.SemaphoreType.DMA((2, 2))],
        ),
        out_shape=jax.ShapeDtypeStruct((n_batch, N_HEADS, rows, HEAD_DIM), F32),
        compiler_params=pltpu.CompilerParams(dimension_semantics=("arbitrary", "arbitrary"),
                                             vmem_limit_bytes=VMEM_LIMIT),
        name="sample_attend",
    )(page_table, sel_packed, wq, kn_t, vn_t, ck_view, cv_view)


def kernel(x_prompt, x_sample, cache_k, cache_v, state_wkv, state_shift, page_table, n1_pre, n1_post, n2_pre, n2_post, w_in, rw_mu, rw_w0, rw_w2, rw_a0, rw_a2, rw_g2, rw_kk, rw_ka, rw_rk, rw_ln_w, rw_ln_b, w_att_br, w_rw_br, w_out, w_up, w_down):
    _, seq, _ = x_prompt.shape
    n_batch, dec_seq, _ = x_sample.shape
    n_pages = page_table.shape[1]
    past = n_pages * PAGE_SIZE
    assert past % MOBA_BLOCK == 0 and n_pages // PAGES_PER_BLOCK >= MOBA_TOPK
    assert n_pages // PAGES_PER_BLOCK <= (1 << SEL_BITS) and x_prompt.shape[0] == 1 and n1_pre.shape[0] == 1
    l = 0
    row2 = lambda a: a[l].reshape(1, -1)
    rwp = dict(rw_mu=row2(rw_mu), rw_w0=row2(rw_w0), rw_w2=rw_w2[l], rw_a0=row2(rw_a0), rw_a2=rw_a2[l],
               rw_g2=rw_g2[l], rw_kk=row2(rw_kk), rw_ka=row2(rw_ka), rw_rk=row2(rw_rk),
               rw_ln_w=row2(rw_ln_w), rw_ln_b=row2(rw_ln_b))
    w_in_b = w_in[l].astype(BF16)
    ffn_w = [w.astype(BF16) for w in (w_att_br[l], w_rw_br[l], w_out[l], w_up[l], w_down[l])]
    norms = [row2(n1_post), row2(n2_pre), row2(n2_post)]
    heads_t = lambda t: t.reshape(N_HEADS, HEAD_DIM, -1)

    xp = x_prompt[0]
    cos, sin = _rope_tables(jnp.arange(seq))
    q, kb, vtb, kt, vt, rw, gate, kmean = _project(xp, cos, sin, row2(n1_pre), w_in_b)
    n_blk = seq // MOBA_BLOCK
    y_att = _moba_prompt(q, kb.reshape(n_blk, MOBA_BLOCK, ATT_W), vtb, kmean.reshape(n_blk, ATT_W))
    prev = jnp.concatenate([jnp.zeros((1, SHIFT_W), F32), rw[:-1]], axis=0)
    y_rw, wkv_p = _rwkv_prompt(rw, prev, rwp)
    y_prompt = _merge_ffn(xp, y_att, y_rw, gate, *ffn_w, *norms)
    k_prompt = heads_t(kt).transpose(0, 2, 1)[None, None]
    v_prompt = heads_t(vt).transpose(0, 2, 1)[None, None]

    xs = x_sample.reshape(n_batch * dec_seq, D_MODEL)
    cos_s, sin_s = _rope_tables(jnp.tile(past + jnp.arange(dec_seq), n_batch))
    qs, _, _, kt_s, vt_s, rw_s, gate_s, _ = _project(xs, cos_s, sin_s, row2(n1_pre), w_in_b)
    kn_t = heads_t(kt_s).reshape(N_HEADS, HEAD_DIM, n_batch, dec_seq)
    vn_t = heads_t(vt_s).reshape(N_HEADS, HEAD_DIM, n_batch, dec_seq)
    q4 = qs.reshape(n_batch, dec_seq, N_HEADS, HEAD_DIM)
    q_t = q4.transpose(0, 3, 2, 1).reshape(n_batch, HEAD_DIM, N_HEADS * dec_seq)
    q_t = jnp.pad(q_t, ((0, 0), (0, 0), (0, LANES - N_HEADS * dec_seq)))
    ck_view = jnp.swapaxes(cache_k[:, l], -1, -2)
    cv_view = jnp.swapaxes(cache_v[:, l], -1, -2)
    sel = _sample_select(page_table, q_t, ck_view, dec_seq)
    sel = sel[:, :MOBA_TOPK, :N_HEADS * dec_seq]
    sel_packed = sel[:, 0] | (sel[:, 1] << SEL_BITS) | (sel[:, 2] << (2 * SEL_BITS))
    wq = jnp.pad(q4.transpose(0, 2, 1, 3) * (HEAD_DIM ** -0.5), ((0, 0), (0, 0), (0, 8 - dec_seq), (0, 0)))
    pad_new = lambda t: jnp.pad(t.transpose(2, 0, 1, 3), ((0, 0), (0, 0), (0, 0), (0, LANES - dec_seq)))
    o_s = _sample_attend(page_table, sel_packed, wq, pad_new(kn_t), pad_new(vn_t), ck_view, cv_view, dec_seq)
    y_att_s = o_s[:, :, :dec_seq].transpose(0, 2, 1, 3).reshape(n_batch * dec_seq, ATT_W).astype(BF16)

    rw_s3 = rw_s.reshape(n_batch, dec_seq, SHIFT_W)
    prev_s = jnp.concatenate([state_shift[:, l][:, None], rw_s3[:, :-1]], axis=1).reshape(n_batch * dec_seq, SHIFT_W)
    to_steps = lambda t: t.reshape(n_batch, dec_seq, RW_HEADS, RW_N).transpose(1, 2, 3, 0)
    vecs = [to_steps(t) for t in _rwkv_mix_call(rw_s, prev_s, rwp)]
    on_lanes = lambda a: jnp.broadcast_to(a[l].reshape(RW_HEADS, RW_N, 1), (RW_HEADS, RW_N, n_batch))
    y_steps, wkv_s = _rwkv_steps(vecs, state_wkv[:, l].transpose(1, 2, 3, 0),
                                 on_lanes(rw_ln_w), on_lanes(rw_ln_b), on_lanes(rw_rk))
    y_rw_s = y_steps.transpose(3, 0, 1, 2).reshape(n_batch * dec_seq, RW_W).astype(BF16)
    y_sample = _merge_ffn(xs, y_att_s, y_rw_s, gate_s, *ffn_w, *norms)
    k_sample = kn_t.transpose(2, 0, 3, 1)[:, None]
    v_sample = vn_t.transpose(2, 0, 3, 1)[:, None]

    return (y_prompt[None], y_sample.reshape(n_batch, dec_seq, D_MODEL), k_prompt, v_prompt, k_sample, v_sample,
            wkv_p[None, None], wkv_s.transpose(3, 0, 1, 2)[:, None], rw[-1][None, None], rw_s3[:, -1][:, None])
```

```python
import functools
import math

import jax
import jax.numpy as jnp
from jax import lax
from jax.experimental import pallas as pl
from jax.experimental.pallas import tpu as pltpu

F32 = jnp.float32
BF16 = jnp.bfloat16

D_MODEL = 1024
N_HEADS = 8
HEAD_DIM = 64
ATT_W = N_HEADS * HEAD_DIM
RW_HEADS = 8
RW_N = 64
RW_W = RW_HEADS * RW_N
DECAY_LORA = 64
AAA_LORA = 64
GATE_LORA = 128
D_FF = 4 * D_MODEL
MOBA_BLOCK = 256
MOBA_TOPK = 3
Q_BLOCK = 128
PAGE_SIZE = 128
ROPE_THETA = 10000.0
NORM_EPS = 1e-6
GN_EPS = 64e-5
NEG_INF = -1e30
LOG2_E = 1.4426950408889634
SHIFT_W = 3 * RW_W + DECAY_LORA + AAA_LORA + GATE_LORA
GATE_W = 2 * D_MODEL
IN_W = 3 * ATT_W + SHIFT_W + GATE_W

LANES = 128
SUBLANES = 8
ONES_ROWS = 16
MOBA_GROUP = 4
VMEM_LIMIT = 56 * 1024 * 1024

NT_DIMS = (((1,), (1,)), ((), ()))
TN_DIMS = (((0,), (0,)), ((), ()))


def _split2(x):
    hi = x.astype(BF16)
    lo = (x - hi.astype(F32)).astype(BF16)
    return hi, lo


def _dot3(a, b, dims=(((1,), (0,)), ((), ()))):
    ah, al = _split2(a)
    bh, bl = _split2(b)
    d = functools.partial(lax.dot_general, dimension_numbers=dims, preferred_element_type=F32)
    return d(ah, bh) + (d(ah, bl) + d(al, bh))


def _split3(x):
    x1 = x.astype(BF16)
    r1 = x - x1.astype(F32)
    x2 = r1.astype(BF16)
    x3 = (r1 - x2.astype(F32)).astype(BF16)
    return x1, x2, x3


def _dot_exact_lhs(mask, b):
    a = mask.astype(BF16)
    b1, b2, b3 = _split3(b)
    d = functools.partial(jnp.dot, preferred_element_type=F32)
    return d(a, b1) + (d(a, b2) + d(a, b3))


def _dot_exact_rhs(a, mask):
    a1, a2, a3 = _split3(a)
    o = mask.astype(BF16)
    d = functools.partial(jnp.dot, preferred_element_type=F32)
    return d(a1, o) + (d(a2, o) + d(a3, o))


def _sigmoid(x):
    return 1.0 / (1.0 + jnp.exp(-x))


def _proj_kernel(x_ref, cos_ref, sin_ref, g_ref, w_ref,
                 q_ref, kb_ref, vtb_ref, kt_ref, vt_ref, rw_ref, gate_ref, kmean_ref, *, tm):
    x = x_ref[...]
    ms = jnp.mean(x * x, axis=-1, keepdims=True)
    h = (x * lax.rsqrt(ms + NORM_EPS) * g_ref[...]).astype(BF16)
    cos = cos_ref[...]
    sin = sin_ref[...]
    lane = lax.broadcasted_iota(jnp.int32, (tm, LANES), 1)
    first_half = (lane % HEAD_DIM) < (HEAD_DIM // 2)

    def rope(p):
        sw = jnp.where(first_half, pltpu.roll(p, LANES - HEAD_DIM // 2, 1), pltpu.roll(p, HEAD_DIM // 2, 1))
        return p * cos + sw * sin

    def proj(lo, width):
        return jnp.dot(h, w_ref[:, lo:lo + width], preferred_element_type=F32)

    pq = proj(0, ATT_W)
    for c in range(ATT_W // LANES):
        q_ref[:, c * LANES:(c + 1) * LANES] = rope(pq[:, c * LANES:(c + 1) * LANES])
    pk = proj(ATT_W, ATT_W)
    for c in range(ATT_W // LANES):
        cols = slice(c * LANES, (c + 1) * LANES)
        kr = rope(pk[:, cols])
        kb_ref[:, cols] = kr.astype(BF16)
        kt_ref[cols, :] = kr.T
        kmean_ref[0, :, cols] = jnp.sum(kr, axis=0, keepdims=True) * (1.0 / tm)
    pv = proj(2 * ATT_W, ATT_W)
    for c in range(ATT_W // LANES):
        cols = slice(c * LANES, (c + 1) * LANES)
        vt = pv[:, cols].T
        vt_ref[cols, :] = vt
        vtb_ref[0, cols, :] = vt.astype(BF16)
    rw_ref[...] = proj(3 * ATT_W, SHIFT_W)
    gate_ref[...] = _sigmoid(proj(3 * ATT_W + SHIFT_W, GATE_W))


def _rope_tables(pos):
    half = HEAD_DIM // 2
    inv = ROPE_THETA ** (-jnp.arange(half, dtype=F32) / half)
    ang = pos.astype(F32)[:, None] * inv[None, :]
    cos = jnp.cos(ang)
    sin = jnp.sin(ang)
    return jnp.tile(jnp.concatenate([cos, cos], axis=1), (1, 2)), jnp.tile(jnp.concatenate([-sin, sin], axis=1), (1, 2))


def _project(x, cos, sin, g, w_in_bf16):
    rows = x.shape[0]
    tm = MOBA_BLOCK
    n = rows // tm
    row = lambda w: pl.BlockSpec((tm, w), lambda i: (i, 0))
    const = lambda a, b: pl.BlockSpec((a, b), lambda i: (0, 0))
    col = pl.BlockSpec((ATT_W, tm), lambda i: (0, i))
    return pl.pallas_call(
        functools.partial(_proj_kernel, tm=tm),
        grid=(n,),
        in_specs=[row(D_MODEL), row(LANES), row(LANES), const(1, D_MODEL), const(D_MODEL, IN_W)],
        out_specs=[row(ATT_W), row(ATT_W), pl.BlockSpec((1, ATT_W, tm), lambda i: (i, 0, 0)), col, col,
                   row(SHIFT_W), row(GATE_W), pl.BlockSpec((1, 1, ATT_W), lambda i: (i, 0, 0))],
        out_shape=[
            jax.ShapeDtypeStruct((rows, ATT_W), F32),
            jax.ShapeDtypeStruct((rows, ATT_W), BF16),
            jax.ShapeDtypeStruct((n, ATT_W, tm), BF16),
            jax.ShapeDtypeStruct((ATT_W, rows), F32),
            jax.ShapeDtypeStruct((ATT_W, rows), F32),
            jax.ShapeDtypeStruct((rows, SHIFT_W), F32),
            jax.ShapeDtypeStruct((rows, GATE_W), F32),
            jax.ShapeDtypeStruct((n, 1, ATT_W), F32),
        ],
        compiler_params=pltpu.CompilerParams(dimension_semantics=("arbitrary",), vmem_limit_bytes=VMEM_LIMIT),
        name="project",
    )(x, cos, sin, g, w_in_bf16)


def _moba_prompt_kernel(q_ref, k_ref, vt_ref, kmean_ref, y_ref, sel_ref, sa_ref, sb_ref, *, n_blk):
    c = pl.program_id(1)
    ob = (c * Q_BLOCK) // MOBA_BLOCK
    q = q_ref[...]
    lane = lax.broadcasted_iota(jnp.int32, (Q_BLOCK, LANES), 1)
    kmean = kmean_ref[...]
    blk = lax.broadcasted_iota(jnp.int32, (n_blk, Q_BLOCK), 0)
    qs = []
    for hh in range(2):
        qh = jnp.where((lane // HEAD_DIM) == hh, q, 0.0)
        qs.append((qh * (HEAD_DIM ** -0.5 * LOG2_E)).astype(BF16))
        sb = _dot3(kmean, qh, NT_DIMS)
        sb = jnp.where(blk < ob, sb, NEG_INF)
        sel = jnp.zeros((n_blk, Q_BLOCK), jnp.bool_)
        for _ in range(MOBA_TOPK):
            m = jnp.max(sb, axis=0, keepdims=True)
            idx = jnp.min(jnp.where(sb == m, blk, n_blk), axis=0, keepdims=True)
            pick = blk == idx
            sel = jnp.logical_or(sel, pick)
            sb = jnp.where(pick, -jnp.inf, sb)
        sel = jnp.logical_and(sel, blk < ob)
        sel_ref[:, hh * Q_BLOCK:(hh + 1) * Q_BLOCK] = sel.astype(F32)
    qs = jnp.concatenate(qs, axis=0)

    def scores(j):
        return lax.dot_general(k_ref[j], qs, NT_DIMS, preferred_element_type=F32)

    s = scores(ob)
    key = lax.broadcasted_iota(jnp.int32, (MOBA_BLOCK, 2 * Q_BLOCK), 0)
    qi = lax.broadcasted_iota(jnp.int32, (MOBA_BLOCK, 2 * Q_BLOCK), 1) % Q_BLOCK
    s = jnp.where(key <= (c % 2) * Q_BLOCK + qi, s, NEG_INF)
    m0 = jnp.max(s, axis=0, keepdims=True)
    ones_rows = jnp.ones((ONES_ROWS, MOBA_BLOCK), BF16)

    def pv(j, p):
        return jnp.dot(jnp.concatenate([vt_ref[j], ones_rows], axis=0), p.astype(BF16), preferred_element_type=F32)

    acc0 = pv(ob, jnp.exp2(s - m0))

    def block(g, t):
        return jnp.minimum(MOBA_GROUP * g + t, n_blk - 1)

    def masked_scores(j):
        return jnp.where(sel_ref[pl.ds(j, 1), :] > 0.0, scores(j), NEG_INF)

    def group_step(g, cur_ref, nxt_ref, m_cur, m_acc, acc):
        mx = jnp.full((1, 2 * Q_BLOCK), NEG_INF, F32)
        tot = None
        for t in range(MOBA_GROUP):
            sn = masked_scores(block(g + 1, t))
            nxt_ref[t] = sn
            mx = jnp.maximum(mx, jnp.max(sn, axis=0, keepdims=True))
            part = pv(block(g, t), jnp.exp2(cur_ref[t] - m_cur))
            tot = part if tot is None else tot + part
        return jnp.maximum(m_cur, mx), m_cur, acc * jnp.exp2(m_acc - m_cur) + tot

    mx = m0
    for t in range(MOBA_GROUP):
        s = masked_scores(block(0, t))
        sa_ref[t] = s
        mx = jnp.maximum(mx, jnp.max(s, axis=0, keepdims=True))

    def body(i, carry):
        m_cur, m_acc, acc = group_step(2 * i, sa_ref, sb_ref, *carry)
        return group_step(2 * i + 1, sb_ref, sa_ref, m_cur, m_acc, acc)

    n_groups = (ob + MOBA_GROUP - 1) // MOBA_GROUP
    _, _, acc = lax.fori_loop(0, (n_groups + 1) // 2, body, (mx, m0, acc0))
    o = acc[:LANES] * (1.0 / acc[LANES:LANES + 1])
    row = lax.broadcasted_iota(jnp.int32, (LANES, Q_BLOCK), 0)
    ot = jnp.where(row < HEAD_DIM, o[:, :Q_BLOCK], o[:, Q_BLOCK:])
    y_ref[...] = ot.T.astype(y_ref.dtype)


def _moba_prompt(q, k_blocks, vt_blocks, kmean):
    s = q.shape[0]
    n_blk = k_blocks.shape[0]
    n_chunk = s // Q_BLOCK
    return pl.pallas_call(
        functools.partial(_moba_prompt_kernel, n_blk=n_blk),
        grid=(ATT_W // LANES, n_chunk),
        in_specs=[
            pl.BlockSpec((Q_BLOCK, LANES), lambda hp, c: (c, hp)),
            pl.BlockSpec((n_blk, MOBA_BLOCK, LANES), lambda hp, c: (0, 0, hp)),
            pl.BlockSpec((n_blk, LANES, MOBA_BLOCK), lambda hp, c: (0, hp, 0)),
            pl.BlockSpec((n_blk, LANES), lambda hp, c: (0, hp)),
        ],
        out_specs=pl.BlockSpec((Q_BLOCK, LANES), lambda hp, c: (c, hp)),
        out_shape=jax.ShapeDtypeStruct((s, ATT_W), BF16),
        scratch_shapes=[pltpu.VMEM((n_blk, 2 * Q_BLOCK), F32),
                        pltpu.VMEM((MOBA_GROUP, MOBA_BLOCK, 2 * Q_BLOCK), F32),
                        pltpu.VMEM((MOBA_GROUP, MOBA_BLOCK, 2 * Q_BLOCK), F32)],
        compiler_params=pltpu.CompilerParams(dimension_semantics=("arbitrary", "arbitrary"),
                                             vmem_limit_bytes=VMEM_LIMIT),
        name="moba_prompt",
    )(q, k_blocks, vt_blocks, kmean)


def _head_ones():
    li = lax.broadcasted_iota(jnp.int32, (RW_W, RW_W), 0) // RW_N
    lj = lax.broadcasted_iota(jnp.int32, (RW_W, RW_W), 1) // RW_N
    return (li == lj).astype(F32)


def _rwkv_mix(rw, prev, mu_ref, w0_ref, w2_ref, a0_ref, a2_ref, g2_ref, kk_ref, ka_ref):
    m = rw + (prev - rw) * mu_ref[...]
    r = m[:, 0:RW_W]
    k = m[:, RW_W:2 * RW_W]
    v = m[:, 2 * RW_W:3 * RW_W]
    o1 = 3 * RW_W
    wd = m[:, o1:o1 + DECAY_LORA]
    ad = m[:, o1 + DECAY_LORA:o1 + DECAY_LORA + AAA_LORA]
    gd = m[:, o1 + DECAY_LORA + AAA_LORA:]
    bdot = lambda a, b: jnp.dot(a.astype(BF16), b.astype(BF16), preferred_element_type=F32)
    w_raw = w0_ref[...] + bdot(jnp.tanh(wd), w2_ref[...])
    logw = -math.exp(-0.5) * _sigmoid(w_raw)
    a_sig = _sigmoid(a0_ref[...] + bdot(ad, a2_ref[...]))
    g = bdot(_sigmoid(gd), g2_ref[...])

    kk = k * kk_ref[...]
    kk_norm = jnp.sqrt(_dot_exact_rhs(kk * kk, _head_ones()))
    kk = kk / jnp.maximum(kk_norm, 1e-12)
    k2 = k * (1.0 + (a_sig - 1.0) * ka_ref[...])
    return r, logw, k2, v, -kk, kk * a_sig, g


def _rwkv_kernel(rw_ref, tail_ref, mu_ref, w0_ref, w2_ref, a0_ref, a2_ref, g2_ref, kk_ref, ka_ref,
                 rk_ref, lnw_ref, lnb_ref, y_ref, s_ref, *, chunk):
    @pl.when(pl.program_id(0) == 0)
    def _():
        s_ref[...] = jnp.zeros_like(s_ref)

    rw = rw_ref[...]
    before = jnp.where(pl.program_id(0) == 0, 0.0, tail_ref[SUBLANES - 1:SUBLANES, :])
    first_row = lax.broadcasted_iota(jnp.int32, rw.shape, 0) == 0
    prev = jnp.where(first_row, before, pltpu.roll(rw, 1, 0))
    r, logw, k2, v, a_vec, b_vec, g = _rwkv_mix(rw, prev, mu_ref, w0_ref, w2_ref, a0_ref, a2_ref,
                                                g2_ref, kk_ref, ka_ref)
    head_ones = _head_ones()
    ti = lax.broadcasted_iota(jnp.int32, (chunk, chunk), 0)
    tj = lax.broadcasted_iota(jnp.int32, (chunk, chunk), 1)
    lower_incl = tj <= ti
    lower_strict = tj < ti
    cs = _dot_exact_lhs(lower_incl.astype(F32), logw)
    p_incl = jnp.exp(cs)
    p_inv = jnp.exp(-cs)
    a_t = a_vec * jnp.exp(cs - logw)
    r_t = r * p_incl
    b_t = b_vec * p_inv
    k_t = k2 * p_inv
    p_last = p_incl[chunk - 1:chunk, :]
    eye = (ti == tj).astype(F32)

    heads = range(RW_HEADS)
    sl = [slice(h * RW_N, (h + 1) * RW_N) for h in heads]
    ar = [jnp.concatenate([a_t[:, s], r_t[:, s]], axis=0) for s in sl]
    bk = [jnp.concatenate([b_t[:, s], k_t[:, s]], axis=0) for s in sl]
    gram = [_dot3(ar[h], bk[h], NT_DIMS) for h in heads]
    s0 = [s_ref[h] for h in heads]
    ars = [_dot3(ar[h], s0[h], NT_DIMS) for h in heads]
    l_ak_v = [_dot3(jnp.where(lower_strict, gram[h][:chunk, chunk:], 0.0), v[:, sl[h]]) for h in heads]
    l_ab = [jnp.where(lower_strict, gram[h][:chunk, :chunk], 0.0) for h in heads]
    x = [eye + l_ab[h] for h in heads]
    pw = [_dot3(l_ab[h], l_ab[h]) for h in heads]
    n_factors = int(math.log2(chunk))
    for it in range(1, n_factors):
        if it + 1 < n_factors:
            prod = [_dot3(pw[h], jnp.concatenate([pw[h], x[h]], axis=1)) for h in heads]
            pw = [prod[h][:, :chunk] for h in heads]
            x = [x[h] + prod[h][:, chunk:] for h in heads]
        else:
            x = [x[h] + _dot3(pw[h], x[h]) for h in heads]
    u = [_dot3(x[h], ars[h][:chunk] + l_ak_v[h]) for h in heads]
    uv = [jnp.concatenate([u[h], v[:, sl[h]]], axis=0) for h in heads]
    wide_i = lax.broadcasted_iota(jnp.int32, (chunk, 2 * chunk), 0)
    wide_j = lax.broadcasted_iota(jnp.int32, (chunk, 2 * chunk), 1) % chunk
    m_r = [jnp.where(wide_j <= wide_i, gram[h][chunk:, :], 0.0) for h in heads]
    ys = [ars[h][chunk:] + _dot3(m_r[h], uv[h]) for h in heads]
    for h in heads:
        s_ref[h] = (s0[h] + _dot3(uv[h], bk[h], TN_DIMS)) * p_last[:, sl[h]]
    o = jnp.concatenate(ys, axis=1)

    mean = _dot_exact_rhs(o, head_ones) * (1.0 / RW_N)
    d = o - mean
    var = _dot_exact_rhs(d * d, head_ones) * (1.0 / RW_N)
    o = d * lax.rsqrt(var + GN_EPS) * lnw_ref[...] + lnb_ref[...]
    bonus = _dot_exact_rhs(r * k2 * rk_ref[...], head_ones) * v
    y_ref[...] = ((o + bonus) * g).astype(y_ref.dtype)


RW_CHUNK = 64

_MIX_PARAMS = ('rw_mu', 'rw_w0', 'rw_w2', 'rw_a0', 'rw_a2', 'rw_g2', 'rw_kk', 'rw_ka')


def _rwkv_prompt(rw, p):
    rows = rw.shape[0]
    rowspec = lambda w: pl.BlockSpec((RW_CHUNK, w), lambda c: (c, 0))
    tail = pl.BlockSpec((SUBLANES, SHIFT_W), lambda c: (jnp.maximum(c * (RW_CHUNK // SUBLANES) - 1, 0), 0))
    const = lambda a: pl.BlockSpec(a.shape, lambda c: (0,) * a.ndim)
    params = [p[n] for n in _MIX_PARAMS] + [p['rw_rk'], p['rw_ln_w'], p['rw_ln_b']]
    return pl.pallas_call(
        functools.partial(_rwkv_kernel, chunk=RW_CHUNK),
        grid=(rows // RW_CHUNK,),
        in_specs=[rowspec(SHIFT_W), tail] + [const(a) for a in params],
        out_specs=[rowspec(RW_W), pl.BlockSpec((RW_HEADS, RW_N, RW_N), lambda c: (0, 0, 0))],
        out_shape=[jax.ShapeDtypeStruct((rows, RW_W), BF16),
                   jax.ShapeDtypeStruct((RW_HEADS, RW_N, RW_N), F32)],
        compiler_params=pltpu.CompilerParams(dimension_semantics=("arbitrary",), vmem_limit_bytes=VMEM_LIMIT),
        name="rwkv7_prompt",
    )(rw, rw, *params)


def _rwkv_mix_kernel(rw_ref, prev_ref, mu_ref, w0_ref, w2_ref, a0_ref, a2_ref, g2_ref, kk_ref, ka_ref,
                     r_ref, w_ref, k_ref, v_ref, a_ref, b_ref, g_ref):
    r, logw, k2, v, a_vec, b_vec, g = _rwkv_mix(rw_ref[...], prev_ref[...], mu_ref, w0_ref, w2_ref, a0_ref, a2_ref,
                                                g2_ref, kk_ref, ka_ref)
    r_ref[...] = r
    w_ref[...] = jnp.exp(logw)
    k_ref[...] = k2
    v_ref[...] = v
    a_ref[...] = a_vec
    b_ref[...] = b_vec
    g_ref[...] = g


def _rwkv_mix_call(rw, prev, p):
    rows = rw.shape[0]
    rowspec = lambda w: pl.BlockSpec((rows, w), lambda i: (0, 0))
    const = lambda a: pl.BlockSpec(a.shape, lambda i: (0,) * a.ndim)
    params = [p[n] for n in _MIX_PARAMS]
    return pl.pallas_call(
        _rwkv_mix_kernel,
        grid=(1,),
        in_specs=[rowspec(SHIFT_W), rowspec(SHIFT_W)] + [const(a) for a in params],
        out_specs=[rowspec(RW_W)] * 7,
        out_shape=[jax.ShapeDtypeStruct((rows, RW_W), F32)] * 7,
        compiler_params=pltpu.CompilerParams(dimension_semantics=("arbitrary",), vmem_limit_bytes=VMEM_LIMIT),
        name="rwkv7_mix",
    )(rw, prev, *params)


def _rwkv_steps_kernel(r_ref, w_ref, k_ref, v_ref, a_ref, b_ref, g_ref, s_ref, lnw_ref, lnb_ref, rk_ref,
                       y_ref, so_ref, o_scr, *, n_steps):
    def value_row(vi, carry):
        s = s_ref[0, vi]
        for t in range(n_steps):
            sa = jnp.sum(s * a_ref[t, 0], axis=0, keepdims=True)
            s = s * w_ref[t, 0] + sa * b_ref[t, 0] + v_ref[t, 0, pl.ds(vi, 1), :] * k_ref[t, 0]
            o_scr[t, pl.ds(vi, 1), :] = jnp.sum(s * r_ref[t, 0], axis=0, keepdims=True)
        so_ref[0, vi] = s
        return carry

    lax.fori_loop(0, RW_N, value_row, 0)
    for t in range(n_steps):
        o = o_scr[t]
        d = o - jnp.mean(o, axis=0, keepdims=True)
        var = jnp.mean(d * d, axis=0, keepdims=True)
        bonus = jnp.sum(r_ref[t, 0] * k_ref[t, 0] * rk_ref[0], axis=0, keepdims=True)
        o = d * lax.rsqrt(var + GN_EPS) * lnw_ref[0] + lnb_ref[0]
        y_ref[t, 0] = (o + bonus * v_ref[t, 0]) * g_ref[t, 0]


def _rwkv_steps(vecs, state, lnw, lnb, rk):
    n_steps, _, _, n_batch = vecs[0].shape
    vec = pl.BlockSpec((n_steps, 1, RW_N, n_batch), lambda h: (0, h, 0, 0))
    st = pl.BlockSpec((1, RW_N, RW_N, n_batch), lambda h: (h, 0, 0, 0))
    par = pl.BlockSpec((1, RW_N, n_batch), lambda h: (h, 0, 0))
    return pl.pallas_call(
        functools.partial(_rwkv_steps_kernel, n_steps=n_steps),
        grid=(RW_HEADS,),
        in_specs=[vec] * 7 + [st, par, par, par],
        out_specs=[vec, st],
        out_shape=[jax.ShapeDtypeStruct(vecs[0].shape, F32), jax.ShapeDtypeStruct(state.shape, F32)],
        scratch_shapes=[pltpu.VMEM((n_steps, RW_N, n_batch), F32)],
        compiler_params=pltpu.CompilerParams(dimension_semantics=("arbitrary",), vmem_limit_bytes=VMEM_LIMIT),
        name="rwkv7_steps",
    )(*vecs, state, lnw, lnb, rk)


def _rms(x, g):
    return x * lax.rsqrt(jnp.mean(x * x, axis=-1, keepdims=True) + NORM_EPS) * g


def _merge_ffn_kernel(x_ref, ya_ref, yr_ref, gate_ref, wa_ref, wr_ref, wo_ref, wu_ref, wd_ref,
                      n1_ref, n2pre_ref, n2_ref, out_ref):
    d = functools.partial(jnp.dot, preferred_element_type=F32)
    merged = gate_ref[:, :D_MODEL] * d(ya_ref[...], wa_ref[...]) + gate_ref[:, D_MODEL:] * d(yr_ref[...], wr_ref[...])
    x1 = x_ref[...] + _rms(d(merged.astype(BF16), wo_ref[...]), n1_ref[...])
    h2 = _rms(x1, n2pre_ref[...]).astype(BF16)
    u = jnp.square(jnp.maximum(d(h2, wu_ref[...]), 0.0))
    out_ref[...] = x1 + _rms(d(u.astype(BF16), wd_ref[...]), n2_ref[...])


def _merge_ffn(x, y_att, y_rw, gates, wa, wr, wo, wu, wd, n1_post, n2_pre, n2_post):
    rows = x.shape[0]
    tm = 256
    row = lambda w: pl.BlockSpec((tm, w), lambda i: (i, 0))
    const = lambda a: pl.BlockSpec(a.shape, lambda i: (0, 0), pipeline_mode=pl.Buffered(1))
    consts = [wa, wr, wo, wu, wd, n1_post, n2_pre, n2_post]
    return pl.pallas_call(
        _merge_ffn_kernel,
        grid=(rows // tm,),
        in_specs=[row(D_MODEL), row(ATT_W), row(RW_W), row(GATE_W)] + [const(a) for a in consts],
        out_specs=row(D_MODEL),
        out_shape=jax.ShapeDtypeStruct((rows, D_MODEL), F32),
        compiler_params=pltpu.CompilerParams(dimension_semantics=("arbitrary",), vmem_limit_bytes=VMEM_LIMIT),
        name="merge_ffn",
    )(x, y_att, y_rw, gates, *consts)


PAGES_PER_BLOCK = MOBA_BLOCK // PAGE_SIZE
WAVE = 16
SELECT_SLOTS = 3
ATTEND_SLOTS = 4
SEL_BITS = 5


def _sample_select_kernel(pt_ref, qt_ref, ck_ref, sel_ref, buf, sem, sc_ref, *, n_pages, n_batch, dec_seq):
    b = pl.program_id(0)
    waves_per_b = n_pages // WAVE
    n_waves = n_batch * waves_per_b
    n_full = n_pages // PAGES_PER_BLOCK

    def copies(g):
        bb, w, slot = g // waves_per_b, g % waves_per_b, g % SELECT_SLOTS
        return [pltpu.make_async_copy(ck_ref.at[pt_ref[bb, w * WAVE + i]], buf.at[slot, i], sem.at[slot])
                for i in range(WAVE)]

    @pl.when(b == 0)
    def _():
        for g in range(SELECT_SLOTS - 1):
            for cp in copies(g):
                cp.start()

    qt = qt_ref[0]
    lane = lax.broadcasted_iota(jnp.int32, (HEAD_DIM, LANES), 1)
    for w in range(waves_per_b):
        g = b * waves_per_b + w
        slot = g % SELECT_SLOTS

        @pl.when(g + SELECT_SLOTS - 1 < n_waves)
        def _():
            for cp in copies(g + SELECT_SLOTS - 1):
                cp.start()

        for cp in copies(g):
            cp.wait()
        for i in range(0, WAVE, PAGES_PER_BLOCK):
            msel = jnp.zeros((HEAD_DIM, LANES), F32)
            for hh in range(N_HEADS):
                t = buf[slot, i, hh]
                for j in range(1, PAGES_PER_BLOCK):
                    t = t + buf[slot, i + j, hh]
                ksum = jnp.sum(t, axis=1, keepdims=True)
                msel = jnp.where(lane // dec_seq == hh, ksum, msel)
            n = (w * WAVE + i) // PAGES_PER_BLOCK
            sc_ref[n:n + 1, :] = jnp.sum(msel * qt, axis=0, keepdims=True) * (1.0 / MOBA_BLOCK)

    sc = sc_ref[...]
    blk = lax.broadcasted_iota(jnp.int32, (n_full, LANES), 0)
    row = lax.broadcasted_iota(jnp.int32, (8, LANES), 0)
    out = jnp.zeros((8, LANES), jnp.int32)
    for r in range(MOBA_TOPK):
        m = jnp.max(sc, axis=0, keepdims=True)
        idx = jnp.min(jnp.where(sc == m, blk, n_full), axis=0, keepdims=True)
        out = jnp.where(row == r, idx, out)
        sc = jnp.where(blk == idx, -jnp.inf, sc)
    sel_ref[0] = out


def _sample_select(page_table, q_t, ck_view, dec_seq):
    n_batch, n_pages = page_table.shape
    n_full = n_pages // PAGES_PER_BLOCK
    return pl.pallas_call(
        functools.partial(_sample_select_kernel, n_pages=n_pages, n_batch=n_batch, dec_seq=dec_seq),
        grid_spec=pltpu.PrefetchScalarGridSpec(
            num_scalar_prefetch=1,
            grid=(n_batch,),
            in_specs=[pl.BlockSpec((1, HEAD_DIM, LANES), lambda b, pt: (b, 0, 0)),
                      pl.BlockSpec(memory_space=pl.ANY)],
            out_specs=pl.BlockSpec((1, 8, LANES), lambda b, pt: (b, 0, 0)),
            scratch_shapes=[pltpu.VMEM((SELECT_SLOTS, WAVE, N_HEADS, HEAD_DIM, PAGE_SIZE), F32),
                            pltpu.SemaphoreType.DMA((SELECT_SLOTS,)),
                            pltpu.VMEM((n_full, LANES), F32)],
        ),
        out_shape=jax.ShapeDtypeStruct((n_batch, 8, LANES), jnp.int32),
        compiler_params=pltpu.CompilerParams(dimension_semantics=("arbitrary",), vmem_limit_bytes=VMEM_LIMIT),
        name="sample_select",
    )(page_table, q_t, ck_view)


def _sample_attend_kernel(pt_ref, sel_ref, wq_ref, kn_ref, vn_ref, ck_ref, cv_ref, o_ref, kbuf, vbuf, sem,
                          *, n_batch, dec_seq):
    b = pl.program_id(0)
    h = pl.program_id(1)
    seg = MOBA_TOPK * MOBA_BLOCK

    def copies(g):
        bb, hh, slot = g // N_HEADS, g % N_HEADS, g % ATTEND_SLOTS
        cps = []
        for qi in range(dec_seq):
            packed = sel_ref[bb, hh * dec_seq + qi]
            for r in range(MOBA_TOPK):
                blk = (packed >> (SEL_BITS * r)) & ((1 << SEL_BITS) - 1)
                for pg in range(PAGES_PER_BLOCK):
                    page = pt_ref[bb, blk * PAGES_PER_BLOCK + pg]
                    t = (qi * MOBA_TOPK + r) * PAGES_PER_BLOCK + pg
                    dst = pl.ds(t * PAGE_SIZE, PAGE_SIZE)
                    cps.append(pltpu.make_async_copy(ck_ref.at[page, hh], kbuf.at[slot, :, dst], sem.at[slot, 0]))
                    cps.append(pltpu.make_async_copy(cv_ref.at[page, hh], vbuf.at[slot, :, dst], sem.at[slot, 1]))
        return cps

    g = b * N_HEADS + h
    slot = g % ATTEND_SLOTS

    @pl.when(g == 0)
    def _():
        for g0 in range(ATTEND_SLOTS - 1):
            for cp in copies(g0):
                cp.start()

    @pl.when(g + ATTEND_SLOTS - 1 < n_batch * N_HEADS)
    def _():
        for cp in copies(g + ATTEND_SLOTS - 1):
            cp.start()

    for cp in copies(g):
        cp.wait()

    wq = wq_ref[0, 0].astype(BF16)
    rows = wq.shape[0]
    s = jnp.dot(wq, kbuf[slot].astype(BF16), preferred_element_type=F32)
    col = lax.broadcasted_iota(jnp.int32, s.shape, 1)
    rr = lax.broadcasted_iota(jnp.int32, s.shape, 0)
    s = jnp.where(col // seg == rr, s, NEG_INF)
    s_own = jnp.dot(wq, kn_ref[0].astype(BF16), preferred_element_type=F32)
    lane = lax.broadcasted_iota(jnp.int32, (rows, LANES), 1)
    r2 = lax.broadcasted_iota(jnp.int32, (rows, LANES), 0)
    own = jnp.logical_and(lane // dec_seq == h, jnp.logical_and(lane % dec_seq <= r2, r2 < dec_seq))
    s_own = jnp.where(own, s_own, NEG_INF)
    m = jnp.maximum(jnp.max(s, axis=1, keepdims=True), jnp.max(s_own, axis=1, keepdims=True))
    p = jnp.exp(s - m)
    p_own = jnp.exp(s_own - m)
    l = jnp.sum(p, axis=1, keepdims=True) + jnp.sum(p_own, axis=1, keepdims=True)
    o = lax.dot_general(p.astype(BF16), vbuf[slot].astype(BF16), NT_DIMS, preferred_element_type=F32)
    o = o + lax.dot_general(p_own.astype(BF16), vn_ref[0].astype(BF16), NT_DIMS, preferred_element_type=F32)
    o_ref[0, 0] = o * (1.0 / l)


def _sample_attend(page_table, sel_packed, wq, kn_t, vn_t, ck_view, cv_view, dec_seq):
    n_batch = page_table.shape[0]
    rows = wq.shape[2]
    n_keys = dec_seq * MOBA_TOPK * MOBA_BLOCK
    qblk = pl.BlockSpec((1, 1, rows, HEAD_DIM), lambda b, h, pt, sel: (b, h, 0, 0))
    nblk = pl.BlockSpec((1, HEAD_DIM, LANES), lambda b, h, pt, sel: (b, 0, 0))
    return pl.pallas_call(
        functools.partial(_sample_attend_kernel, n_batch=n_batch, dec_seq=dec_seq),
        grid_spec=pltpu.PrefetchScalarGridSpec(
            num_scalar_prefetch=2,
            grid=(n_batch, N_HEADS),
            in_specs=[qblk, nblk, nblk, pl.BlockSpec(memory_space=pl.ANY), pl.BlockSpec(memory_space=pl.ANY)],
            out_specs=qblk,
            scratch_shapes=[pltpu.VMEM((ATTEND_SLOTS, HEAD_DIM, n_keys), F32),
                            pltpu.VMEM((ATTEND_SLOTS, HEAD_DIM, n_keys), F32),
                            pltpu.SemaphoreType.DMA((ATTEND_SLOTS, 2))],
        ),
        out_shape=jax.ShapeDtypeStruct((n_batch, N_HEADS, rows, HEAD_DIM), F32),
        compiler_params=pltpu.CompilerParams(dimension_semantics=("arbitrary", "arbitrary"),
                                             vmem_limit_bytes=VMEM_LIMIT),
        name="sample_attend",
    )(page_table, sel_packed, wq, kn_t, vn_t, ck_view, cv_view)


def kernel(x_prompt, x_sample, cache_k, cache_v, state_wkv, state_shift, page_table, n1_pre, n1_post, n2_pre, n2_post, w_in, rw_mu, rw_w0, rw_w2, rw_a0, rw_a2, rw_g2, rw_kk, rw_ka, rw_rk, rw_ln_w, rw_ln_b, w_att_br, w_rw_br, w_out, w_up, w_down):
    _, seq, _ = x_prompt.shape
    n_batch, dec_seq, _ = x_sample.shape
    n_pages = page_table.shape[1]
    past = n_pages * PAGE_SIZE
    assert past % MOBA_BLOCK == 0 and n_pages // PAGES_PER_BLOCK >= MOBA_TOPK
    assert n_pages // PAGES_PER_BLOCK <= (1 << SEL_BITS) and x_prompt.shape[0] == 1 and n1_pre.shape[0] == 1
    l = 0
    row2 = lambda a: a[l].reshape(1, -1)
    rwp = dict(rw_mu=row2(rw_mu), rw_w0=row2(rw_w0), rw_w2=rw_w2[l], rw_a0=row2(rw_a0), rw_a2=rw_a2[l],
               rw_g2=rw_g2[l], rw_kk=row2(rw_kk), rw_ka=row2(rw_ka), rw_rk=row2(rw_rk),
               rw_ln_w=row2(rw_ln_w), rw_ln_b=row2(rw_ln_b))
    w_in_b = w_in[l].astype(BF16)
    ffn_w = [w.astype(BF16) for w in (w_att_br[l], w_rw_br[l], w_out[l], w_up[l], w_down[l])]
    norms = [row2(n1_post), row2(n2_pre), row2(n2_post)]
    heads_t = lambda t: t.reshape(N_HEADS, HEAD_DIM, -1)

    xp = x_prompt[0]
    cos, sin = _rope_tables(jnp.arange(seq))
    q, kb, vtb, kt, vt, rw, gate, kmean = _project(xp, cos, sin, row2(n1_pre), w_in_b)
    n_blk = seq // MOBA_BLOCK
    y_att = _moba_prompt(q, kb.reshape(n_blk, MOBA_BLOCK, ATT_W), vtb, kmean.reshape(n_blk, ATT_W))
    y_rw, wkv_p = _rwkv_prompt(rw, rwp)
    y_prompt = _merge_ffn(xp, y_att, y_rw, gate, *ffn_w, *norms)
    k_prompt = heads_t(kt).transpose(0, 2, 1)[None, None]
    v_prompt = heads_t(vt).transpose(0, 2, 1)[None, None]

    xs = x_sample.reshape(n_batch * dec_seq, D_MODEL)
    cos_s, sin_s = _rope_tables(jnp.tile(past + jnp.arange(dec_seq), n_batch))
    qs, _, _, kt_s, vt_s, rw_s, gate_s, _ = _project(xs, cos_s, sin_s, row2(n1_pre), w_in_b)
    kn_t = heads_t(kt_s).reshape(N_HEADS, HEAD_DIM, n_batch, dec_seq)
    vn_t = heads_t(vt_s).reshape(N_HEADS, HEAD_DIM, n_batch, dec_seq)
    q4 = qs.reshape(n_batch, dec_seq, N_HEADS, HEAD_DIM)
    q_t = q4.transpose(0, 3, 2, 1).reshape(n_batch, HEAD_DIM, N_HEADS * dec_seq)
    q_t = jnp.pad(q_t, ((0, 0), (0, 0), (0, LANES - N_HEADS * dec_seq)))
    ck_view = jnp.swapaxes(cache_k[:, l], -1, -2)
    cv_view = jnp.swapaxes(cache_v[:, l], -1, -2)
    sel = _sample_select(page_table, q_t, ck_view, dec_seq)
    sel = sel[:, :MOBA_TOPK, :N_HEADS * dec_seq]
    sel_packed = sel[:, 0] | (sel[:, 1] << SEL_BITS) | (sel[:, 2] << (2 * SEL_BITS))
    wq = jnp.pad(q4.transpose(0, 2, 1, 3) * (HEAD_DIM ** -0.5), ((0, 0), (0, 0), (0, 8 - dec_seq), (0, 0)))
    pad_new = lambda t: jnp.pad(t.transpose(2, 1, 0, 3).reshape(n_batch, HEAD_DIM, N_HEADS * dec_seq),
                                ((0, 0), (0, 0), (0, LANES - N_HEADS * dec_seq)))
    o_s = _sample_attend(page_table, sel_packed, wq, pad_new(kn_t), pad_new(vn_t), ck_view, cv_view, dec_seq)
    y_att_s = o_s[:, :, :dec_seq].transpose(0, 2, 1, 3).reshape(n_batch * dec_seq, ATT_W).astype(BF16)

    rw_s3 = rw_s.reshape(n_batch, dec_seq, SHIFT_W)
    prev_s = jnp.concatenate([state_shift[:, l][:, None], rw_s3[:, :-1]], axis=1).reshape(n_batch * dec_seq, SHIFT_W)
    to_steps = lambda t: t.reshape(n_batch, dec_seq, RW_HEADS, RW_N).transpose(1, 2, 3, 0)
    vecs = [to_steps(t) for t in _rwkv_mix_call(rw_s, prev_s, rwp)]
    on_lanes = lambda a: jnp.broadcast_to(a[l].reshape(RW_HEADS, RW_N, 1), (RW_HEADS, RW_N, n_batch))
    y_steps, wkv_s = _rwkv_steps(vecs, state_wkv[:, l].transpose(1, 2, 3, 0),
                                 on_lanes(rw_ln_w), on_lanes(rw_ln_b), on_lanes(rw_rk))
    y_rw_s = y_steps.transpose(3, 0, 1, 2).reshape(n_batch * dec_seq, RW_W).astype(BF16)
    y_sample = _merge_ffn(xs, y_att_s, y_rw_s, gate_s, *ffn_w, *norms)
    k_sample = kn_t.transpose(2, 0, 3, 1)[:, None]
    v_sample = vn_t.transpose(2, 0, 3, 1)[:, None]

    return (y_prompt[None], y_sample.reshape(n_batch, dec_seq, D_MODEL), k_prompt, v_prompt, k_sample, v_sample,
            wkv_p[None, None], wkv_s.transpose(3, 0, 1, 2)[:, None], rw[-1][None, None], rw_s3[:, -1][:, None])
```

```python
import functools
import math

import jax
import jax.numpy as jnp
from jax import lax
from jax.experimental import pallas as pl
from jax.experimental.pallas import tpu as pltpu

F32 = jnp.float32
BF16 = jnp.bfloat16

D_MODEL = 1024
N_HEADS = 8
HEAD_DIM = 64
ATT_W = N_HEADS * HEAD_DIM
RW_HEADS = 8
RW_N = 64
RW_W = RW_HEADS * RW_N
DECAY_LORA = 64
AAA_LORA = 64
GATE_LORA = 128
D_FF = 4 * D_MODEL
MOBA_BLOCK = 256
MOBA_TOPK = 3
Q_BLOCK = 128
PAGE_SIZE = 128
ROPE_THETA = 10000.0
NORM_EPS = 1e-6
GN_EPS = 64e-5
NEG_INF = -1e30
LOG2_E = 1.4426950408889634
SHIFT_W = 3 * RW_W + DECAY_LORA + AAA_LORA + GATE_LORA
GATE_W = 2 * D_MODEL
IN_W = 3 * ATT_W + SHIFT_W + GATE_W

LANES = 128
SUBLANES = 8
ONES_ROWS = 16
MOBA_GROUP = 4
MOBA_COLS = (MOBA_BLOCK // Q_BLOCK) * 2 * Q_BLOCK
VMEM_LIMIT = 56 * 1024 * 1024

NT_DIMS = (((1,), (1,)), ((), ()))
TN_DIMS = (((0,), (0,)), ((), ()))


def _split2(x):
    hi = x.astype(BF16)
    lo = (x - hi.astype(F32)).astype(BF16)
    return hi, lo


def _dot3(a, b, dims=(((1,), (0,)), ((), ()))):
    ah, al = _split2(a)
    bh, bl = _split2(b)
    d = functools.partial(lax.dot_general, dimension_numbers=dims, preferred_element_type=F32)
    return d(ah, bh) + (d(ah, bl) + d(al, bh))


def _split3(x):
    x1 = x.astype(BF16)
    r1 = x - x1.astype(F32)
    x2 = r1.astype(BF16)
    x3 = (r1 - x2.astype(F32)).astype(BF16)
    return x1, x2, x3


def _dot_exact_lhs(mask, b):
    a = mask.astype(BF16)
    b1, b2, b3 = _split3(b)
    d = functools.partial(jnp.dot, preferred_element_type=F32)
    return d(a, b1) + (d(a, b2) + d(a, b3))


def _head_sums(x):
    li = lax.broadcasted_iota(jnp.int32, (LANES, LANES), 0) // RW_N
    lj = lax.broadcasted_iota(jnp.int32, (LANES, LANES), 1) // RW_N
    ones = (li == lj).astype(BF16)
    d = functools.partial(jnp.dot, preferred_element_type=F32)
    out = []
    for c in range(RW_W // LANES):
        hi, lo = _split2(x[:, c * LANES:(c + 1) * LANES])
        out.append(d(hi, ones) + d(lo, ones))
    return jnp.concatenate(out, axis=1)


def _sigmoid(x):
    return 1.0 / (1.0 + jnp.exp(-x))


def _proj_kernel(x_ref, cos_ref, sin_ref, g_ref, w_ref,
                 q_ref, kb_ref, vtb_ref, kt_ref, vt_ref, rw_ref, gate_ref, kmean_ref, *, tm):
    x = x_ref[...]
    ms = jnp.mean(x * x, axis=-1, keepdims=True)
    h = (x * lax.rsqrt(ms + NORM_EPS) * g_ref[...]).astype(BF16)
    cos = cos_ref[...]
    sin = sin_ref[...]
    lane = lax.broadcasted_iota(jnp.int32, (tm, LANES), 1)
    first_half = (lane % HEAD_DIM) < (HEAD_DIM // 2)

    def rope(p):
        sw = jnp.where(first_half, pltpu.roll(p, LANES - HEAD_DIM // 2, 1), pltpu.roll(p, HEAD_DIM // 2, 1))
        return p * cos + sw * sin

    def proj(lo, width):
        return jnp.dot(h, w_ref[:, lo:lo + width], preferred_element_type=F32)

    pq = proj(0, ATT_W)
    for c in range(ATT_W // LANES):
        q_ref[:, c * LANES:(c + 1) * LANES] = rope(pq[:, c * LANES:(c + 1) * LANES])
    pk = proj(ATT_W, ATT_W)
    for c in range(ATT_W // LANES):
        cols = slice(c * LANES, (c + 1) * LANES)
        kr = rope(pk[:, cols])
        kb_ref[:, cols] = kr.astype(BF16)
        kt_ref[cols, :] = kr.T
        kmean_ref[0, :, cols] = jnp.sum(kr, axis=0, keepdims=True) * (1.0 / tm)
    pv = proj(2 * ATT_W, ATT_W)
    for c in range(ATT_W // LANES):
        cols = slice(c * LANES, (c + 1) * LANES)
        vt = pv[:, cols].T
        vt_ref[cols, :] = vt
        vtb_ref[0, cols, :] = vt.astype(BF16)
    rw_ref[...] = proj(3 * ATT_W, SHIFT_W)
    gate_ref[...] = _sigmoid(proj(3 * ATT_W + SHIFT_W, GATE_W))


def _rope_tables(pos):
    half = HEAD_DIM // 2
    inv = ROPE_THETA ** (-jnp.arange(half, dtype=F32) / half)
    ang = pos.astype(F32)[:, None] * inv[None, :]
    cos = jnp.cos(ang)
    sin = jnp.sin(ang)
    return jnp.tile(jnp.concatenate([cos, cos], axis=1), (1, 2)), jnp.tile(jnp.concatenate([-sin, sin], axis=1), (1, 2))


def _project(x, cos, sin, g, w_in_bf16):
    rows = x.shape[0]
    tm = MOBA_BLOCK
    n = rows // tm
    row = lambda w: pl.BlockSpec((tm, w), lambda i: (i, 0))
    const = lambda a, b: pl.BlockSpec((a, b), lambda i: (0, 0))
    col = pl.BlockSpec((ATT_W, tm), lambda i: (0, i))
    return pl.pallas_call(
        functools.partial(_proj_kernel, tm=tm),
        grid=(n,),
        in_specs=[row(D_MODEL), row(LANES), row(LANES), const(1, D_MODEL), const(D_MODEL, IN_W)],
        out_specs=[row(ATT_W), row(ATT_W), pl.BlockSpec((1, ATT_W, tm), lambda i: (i, 0, 0)), col, col,
                   row(SHIFT_W), row(GATE_W), pl.BlockSpec((1, 1, ATT_W), lambda i: (i, 0, 0))],
        out_shape=[
            jax.ShapeDtypeStruct((rows, ATT_W), F32),
            jax.ShapeDtypeStruct((rows, ATT_W), BF16),
            jax.ShapeDtypeStruct((n, ATT_W, tm), BF16),
            jax.ShapeDtypeStruct((ATT_W, rows), F32),
            jax.ShapeDtypeStruct((ATT_W, rows), F32),
            jax.ShapeDtypeStruct((rows, SHIFT_W), F32),
            jax.ShapeDtypeStruct((rows, GATE_W), F32),
            jax.ShapeDtypeStruct((n, 1, ATT_W), F32),
        ],
        compiler_params=pltpu.CompilerParams(dimension_semantics=("arbitrary",), vmem_limit_bytes=VMEM_LIMIT),
        name="project",
    )(x, cos, sin, g, w_in_bf16)


def _moba_prompt_kernel(q_ref, k_ref, vt_ref, kmean_ref, y_ref, sel_ref, sa_ref, sb_ref, *, n_blk):
    ob = pl.program_id(1)
    lane = lax.broadcasted_iota(jnp.int32, (Q_BLOCK, LANES), 1)
    kmean = kmean_ref[...]
    blk = lax.broadcasted_iota(jnp.int32, (n_blk, Q_BLOCK), 0)
    qs = []
    for ch in range(MOBA_BLOCK // Q_BLOCK):
        q = q_ref[ch * Q_BLOCK:(ch + 1) * Q_BLOCK, :]
        for hh in range(2):
            qh = jnp.where((lane // HEAD_DIM) == hh, q, 0.0)
            qs.append((qh * (HEAD_DIM ** -0.5 * LOG2_E)).astype(BF16))
            sb = _dot3(kmean, qh, NT_DIMS)
            sb = jnp.where(blk < ob, sb, NEG_INF)
            sel = jnp.zeros((n_blk, Q_BLOCK), jnp.bool_)
            for _ in range(MOBA_TOPK):
                m = jnp.max(sb, axis=0, keepdims=True)
                idx = jnp.min(jnp.where(sb == m, blk, n_blk), axis=0, keepdims=True)
                pick = blk == idx
                sel = jnp.logical_or(sel, pick)
                sb = jnp.where(pick, -jnp.inf, sb)
            sel = jnp.logical_and(sel, blk < ob)
            col0 = (2 * ch + hh) * Q_BLOCK
            sel_ref[:, col0:col0 + Q_BLOCK] = sel.astype(F32)
    qs = jnp.concatenate(qs, axis=0)

    def scores(j):
        return lax.dot_general(k_ref[j], qs, NT_DIMS, preferred_element_type=F32)

    s = scores(ob)
    key = lax.broadcasted_iota(jnp.int32, (MOBA_BLOCK, MOBA_COLS), 0)
    col = lax.broadcasted_iota(jnp.int32, (MOBA_BLOCK, MOBA_COLS), 1)
    s = jnp.where(key <= (col // (2 * Q_BLOCK)) * Q_BLOCK + col % Q_BLOCK, s, NEG_INF)
    m0 = jnp.max(s, axis=0, keepdims=True)
    ones_rows = jnp.ones((ONES_ROWS, MOBA_BLOCK), BF16)

    def pv(j, p):
        return jnp.dot(jnp.concatenate([vt_ref[j], ones_rows], axis=0), p.astype(BF16), preferred_element_type=F32)

    acc0 = pv(ob, jnp.exp2(s - m0))

    def block(g, t):
        return jnp.minimum(MOBA_GROUP * g + t, n_blk - 1)

    def masked_scores(j):
        return jnp.where(sel_ref[pl.ds(j, 1), :] > 0.0, scores(j), NEG_INF)

    def group_step(g, cur_ref, nxt_ref, m_cur, m_acc, acc):
        mx = jnp.full((1, MOBA_COLS), NEG_INF, F32)
        tot = None
        for t in range(MOBA_GROUP):
            sn = masked_scores(block(g + 1, t))
            nxt_ref[t] = sn
            mx = jnp.maximum(mx, jnp.max(sn, axis=0, keepdims=True))
            part = pv(block(g, t), jnp.exp2(cur_ref[t] - m_cur))
            tot = part if tot is None else tot + part
        return jnp.maximum(m_cur, mx), m_cur, acc * jnp.exp2(m_acc - m_cur) + tot

    mx = m0
    for t in range(MOBA_GROUP):
        s = masked_scores(block(0, t))
        sa_ref[t] = s
        mx = jnp.maximum(mx, jnp.max(s, axis=0, keepdims=True))

    def body(i, carry):
        m_cur, m_acc, acc = group_step(2 * i, sa_ref, sb_ref, *carry)
        return group_step(2 * i + 1, sb_ref, sa_ref, m_cur, m_acc, acc)

    n_groups = (ob + MOBA_GROUP - 1) // MOBA_GROUP
    _, _, acc = lax.fori_loop(0, (n_groups + 1) // 2, body, (mx, m0, acc0))
    o = acc[:LANES] * (1.0 / acc[LANES:LANES + 1])
    row = lax.broadcasted_iota(jnp.int32, (LANES, Q_BLOCK), 0)
    for ch in range(MOBA_BLOCK // Q_BLOCK):
        c0 = 2 * ch * Q_BLOCK
        ot = jnp.where(row < HEAD_DIM, o[:, c0:c0 + Q_BLOCK], o[:, c0 + Q_BLOCK:c0 + 2 * Q_BLOCK])
        y_ref[ch * Q_BLOCK:(ch + 1) * Q_BLOCK, :] = ot.T.astype(y_ref.dtype)


def _moba_prompt(q, k_blocks, vt_blocks, kmean):
    s = q.shape[0]
    n_blk = k_blocks.shape[0]
    return pl.pallas_call(
        functools.partial(_moba_prompt_kernel, n_blk=n_blk),
        grid=(ATT_W // LANES, n_blk),
        in_specs=[
            pl.BlockSpec((MOBA_BLOCK, LANES), lambda hp, c: (c, hp)),
            pl.BlockSpec((n_blk, MOBA_BLOCK, LANES), lambda hp, c: (0, 0, hp)),
            pl.BlockSpec((n_blk, LANES, MOBA_BLOCK), lambda hp, c: (0, hp, 0)),
            pl.BlockSpec((n_blk, LANES), lambda hp, c: (0, hp)),
        ],
        out_specs=pl.BlockSpec((MOBA_BLOCK, LANES), lambda hp, c: (c, hp)),
        out_shape=jax.ShapeDtypeStruct((s, ATT_W), BF16),
        scratch_shapes=[pltpu.VMEM((n_blk, MOBA_COLS), F32),
                        pltpu.VMEM((MOBA_GROUP, MOBA_BLOCK, MOBA_COLS), F32),
                        pltpu.VMEM((MOBA_GROUP, MOBA_BLOCK, MOBA_COLS), F32)],
        compiler_params=pltpu.CompilerParams(dimension_semantics=("arbitrary", "arbitrary"),
                                             vmem_limit_bytes=VMEM_LIMIT),
        name="moba_prompt",
    )(q, k_blocks, vt_blocks, kmean)


def _rwkv_mix(rw, prev, mu_ref, w0_ref, w2_ref, a0_ref, a2_ref, g2_ref, kk_ref, ka_ref):
    m = rw + (prev - rw) * mu_ref[...]
    r = m[:, 0:RW_W]
    k = m[:, RW_W:2 * RW_W]
    v = m[:, 2 * RW_W:3 * RW_W]
    o1 = 3 * RW_W
    wd = m[:, o1:o1 + DECAY_LORA]
    ad = m[:, o1 + DECAY_LORA:o1 + DECAY_LORA + AAA_LORA]
    gd = m[:, o1 + DECAY_LORA + AAA_LORA:]
    bdot = lambda a, b: jnp.dot(a.astype(BF16), b.astype(BF16), preferred_element_type=F32)
    w_raw = w0_ref[...] + bdot(jnp.tanh(wd), w2_ref[...])
    logw = -math.exp(-0.5) * _sigmoid(w_raw)
    a_sig = _sigmoid(a0_ref[...] + bdot(ad, a2_ref[...]))
    g = bdot(_sigmoid(gd), g2_ref[...])

    kk = k * kk_ref[...]
    kk_norm = jnp.sqrt(_head_sums(kk * kk))
    kk = kk / jnp.maximum(kk_norm, 1e-12)
    k2 = k * (1.0 + (a_sig - 1.0) * ka_ref[...])
    return r, logw, k2, v, -kk, kk * a_sig, g


def _rwkv_kernel(rw_ref, tail_ref, mu_ref, w0_ref, w2_ref, a0_ref, a2_ref, g2_ref, kk_ref, ka_ref,
                 rk_ref, lnw_ref, lnb_ref, y_ref, s_ref, *, chunk):
    @pl.when(pl.program_id(0) == 0)
    def _():
        s_ref[...] = jnp.zeros_like(s_ref)

    rw = rw_ref[...]
    before = jnp.where(pl.program_id(0) == 0, 0.0, tail_ref[SUBLANES - 1:SUBLANES, :])
    first_row = lax.broadcasted_iota(jnp.int32, rw.shape, 0) == 0
    prev = jnp.where(first_row, before, pltpu.roll(rw, 1, 0))
    r, logw, k2, v, a_vec, b_vec, g = _rwkv_mix(rw, prev, mu_ref, w0_ref, w2_ref, a0_ref, a2_ref,
                                                g2_ref, kk_ref, ka_ref)
    ti = lax.broadcasted_iota(jnp.int32, (chunk, chunk), 0)
    tj = lax.broadcasted_iota(jnp.int32, (chunk, chunk), 1)
    lower_incl = tj <= ti
    lower_strict = tj < ti
    cs = _dot_exact_lhs(lower_incl.astype(F32), logw)
    p_incl = jnp.exp(cs)
    p_inv = jnp.exp(-cs)
    a_t = a_vec * jnp.exp(cs - logw)
    r_t = r * p_incl
    b_t = b_vec * p_inv
    k_t = k2 * p_inv
    p_last = p_incl[chunk - 1:chunk, :]
    eye = (ti == tj).astype(F32)

    heads = range(RW_HEADS)
    sl = [slice(h * RW_N, (h + 1) * RW_N) for h in heads]
    ar = [jnp.concatenate([a_t[:, s], r_t[:, s]], axis=0) for s in sl]
    bk = [jnp.concatenate([b_t[:, s], k_t[:, s]], axis=0) for s in sl]
    gram = [_dot3(ar[h], bk[h], NT_DIMS) for h in heads]
    s0 = [s_ref[h] for h in heads]
    ars = [_dot3(ar[h], s0[h], NT_DIMS) for h in heads]
    l_ak_v = [_dot3(jnp.where(lower_strict, gram[h][:chunk, chunk:], 0.0), v[:, sl[h]]) for h in heads]
    l_ab = [jnp.where(lower_strict, gram[h][:chunk, :chunk], 0.0) for h in heads]
    x = [eye + l_ab[h] for h in heads]
    pw = [_dot3(l_ab[h], l_ab[h]) for h in heads]
    n_factors = int(math.log2(chunk))
    for it in range(1, n_factors):
        if it + 1 < n_factors:
            prod = [_dot3(pw[h], jnp.concatenate([pw[h], x[h]], axis=1)) for h in heads]
            pw = [prod[h][:, :chunk] for h in heads]
            x = [x[h] + prod[h][:, chunk:] for h in heads]
        else:
            x = [x[h] + _dot3(pw[h], x[h]) for h in heads]
    u = [_dot3(x[h], ars[h][:chunk] + l_ak_v[h]) for h in heads]
    uv = [jnp.concatenate([u[h], v[:, sl[h]]], axis=0) for h in heads]
    wide_i = lax.broadcasted_iota(jnp.int32, (chunk, 2 * chunk), 0)
    wide_j = lax.broadcasted_iota(jnp.int32, (chunk, 2 * chunk), 1) % chunk
    m_r = [jnp.where(wide_j <= wide_i, gram[h][chunk:, :], 0.0) for h in heads]
    ys = [ars[h][chunk:] + _dot3(m_r[h], uv[h]) for h in heads]
    for h in heads:
        s_ref[h] = (s0[h] + _dot3(uv[h], bk[h], TN_DIMS)) * p_last[:, sl[h]]
    o = jnp.concatenate(ys, axis=1)

    mean = _head_sums(o) * (1.0 / RW_N)
    d = o - mean
    var = _head_sums(d * d) * (1.0 / RW_N)
    o = d * lax.rsqrt(var + GN_EPS) * lnw_ref[...] + lnb_ref[...]
    bonus = _head_sums(r * k2 * rk_ref[...]) * v
    y_ref[...] = ((o + bonus) * g).astype(y_ref.dtype)


RW_CHUNK = 64

_MIX_PARAMS = ('rw_mu', 'rw_w0', 'rw_w2', 'rw_a0', 'rw_a2', 'rw_g2', 'rw_kk', 'rw_ka')


def _rwkv_prompt(rw, p):
    rows = rw.shape[0]
    rowspec = lambda w: pl.BlockSpec((RW_CHUNK, w), lambda c: (c, 0))
    tail = pl.BlockSpec((SUBLANES, SHIFT_W), lambda c: (jnp.maximum(c * (RW_CHUNK // SUBLANES) - 1, 0), 0))
    const = lambda a: pl.BlockSpec(a.shape, lambda c: (0,) * a.ndim)
    params = [p[n] for n in _MIX_PARAMS] + [p['rw_rk'], p['rw_ln_w'], p['rw_ln_b']]
    return pl.pallas_call(
        functools.partial(_rwkv_kernel, chunk=RW_CHUNK),
        grid=(rows // RW_CHUNK,),
        in_specs=[rowspec(SHIFT_W), tail] + [const(a) for a in params],
        out_specs=[rowspec(RW_W), pl.BlockSpec((RW_HEADS, RW_N, RW_N), lambda c: (0, 0, 0))],
        out_shape=[jax.ShapeDtypeStruct((rows, RW_W), BF16),
                   jax.ShapeDtypeStruct((RW_HEADS, RW_N, RW_N), F32)],
        compiler_params=pltpu.CompilerParams(dimension_semantics=("arbitrary",), vmem_limit_bytes=VMEM_LIMIT),
        name="rwkv7_prompt",
    )(rw, rw, *params)


def _rwkv_mix_kernel(rw_ref, prev_ref, mu_ref, w0_ref, w2_ref, a0_ref, a2_ref, g2_ref, kk_ref, ka_ref,
                     r_ref, w_ref, k_ref, v_ref, a_ref, b_ref, g_ref):
    r, logw, k2, v, a_vec, b_vec, g = _rwkv_mix(rw_ref[...], prev_ref[...], mu_ref, w0_ref, w2_ref, a0_ref, a2_ref,
                                                g2_ref, kk_ref, ka_ref)
    r_ref[...] = r
    w_ref[...] = jnp.exp(logw)
    k_ref[...] = k2
    v_ref[...] = v
    a_ref[...] = a_vec
    b_ref[...] = b_vec
    g_ref[...] = g


def _rwkv_mix_call(rw, prev, p):
    rows = rw.shape[0]
    rowspec = lambda w: pl.BlockSpec((rows, w), lambda i: (0, 0))
    const = lambda a: pl.BlockSpec(a.shape, lambda i: (0,) * a.ndim)
    params = [p[n] for n in _MIX_PARAMS]
    return pl.pallas_call(
        _rwkv_mix_kernel,
        grid=(1,),
        in_specs=[rowspec(SHIFT_W), rowspec(SHIFT_W)] + [const(a) for a in params],
        out_specs=[rowspec(RW_W)] * 7,
        out_shape=[jax.ShapeDtypeStruct((rows, RW_W), F32)] * 7,
        compiler_params=pltpu.CompilerParams(dimension_semantics=("arbitrary",), vmem_limit_bytes=VMEM_LIMIT),
        name="rwkv7_mix",
    )(rw, prev, *params)


def _rwkv_steps_kernel(r_ref, w_ref, k_ref, v_ref, a_ref, b_ref, g_ref, s_ref, lnw_ref, lnb_ref, rk_ref,
                       y_ref, so_ref, o_scr, *, n_steps):
    def value_row(vi, carry):
        s = s_ref[0, vi]
        for t in range(n_steps):
            sa = jnp.sum(s * a_ref[t, 0], axis=0, keepdims=True)
            s = s * w_ref[t, 0] + sa * b_ref[t, 0] + v_ref[t, 0, pl.ds(vi, 1), :] * k_ref[t, 0]
            o_scr[t, pl.ds(vi, 1), :] = jnp.sum(s * r_ref[t, 0], axis=0, keepdims=True)
        so_ref[0, vi] = s
        return carry

    lax.fori_loop(0, RW_N, value_row, 0)
    for t in range(n_steps):
        o = o_scr[t]
        d = o - jnp.mean(o, axis=0, keepdims=True)
        var = jnp.mean(d * d, axis=0, keepdims=True)
        bonus = jnp.sum(r_ref[t, 0] * k_ref[t, 0] * rk_ref[0], axis=0, keepdims=True)
        o = d * lax.rsqrt(var + GN_EPS) * lnw_ref[0] + lnb_ref[0]
        y_ref[t, 0] = (o + bonus * v_ref[t, 0]) * g_ref[t, 0]


def _rwkv_steps(vecs, state, lnw, lnb, rk):
    n_steps, _, _, n_batch = vecs[0].shape
    vec = pl.BlockSpec((n_steps, 1, RW_N, n_batch), lambda h: (0, h, 0, 0))
    st = pl.BlockSpec((1, RW_N, RW_N, n_batch), lambda h: (h, 0, 0, 0))
    par = pl.BlockSpec((1, RW_N, n_batch), lambda h: (h, 0, 0))
    return pl.pallas_call(
        functools.partial(_rwkv_steps_kernel, n_steps=n_steps),
        grid=(RW_HEADS,),
        in_specs=[vec] * 7 + [st, par, par, par],
        out_specs=[vec, st],
        out_shape=[jax.ShapeDtypeStruct(vecs[0].shape, F32), jax.ShapeDtypeStruct(state.shape, F32)],
        scratch_shapes=[pltpu.VMEM((n_steps, RW_N, n_batch), F32)],
        compiler_params=pltpu.CompilerParams(dimension_semantics=("arbitrary",), vmem_limit_bytes=VMEM_LIMIT),
        name="rwkv7_steps",
    )(*vecs, state, lnw, lnb, rk)


def _rms(x, g):
    return x * lax.rsqrt(jnp.mean(x * x, axis=-1, keepdims=True) + NORM_EPS) * g


def _merge_ffn_kernel(x_ref, ya_ref, yr_ref, gate_ref, wa_ref, wr_ref, wo_ref, wu_ref, wd_ref,
                      n1_ref, n2pre_ref, n2_ref, out_ref):
    d = functools.partial(jnp.dot, preferred_element_type=F32)
    merged = gate_ref[:, :D_MODEL] * d(ya_ref[...], wa_ref[...]) + gate_ref[:, D_MODEL:] * d(yr_ref[...], wr_ref[...])
    x1 = x_ref[...] + _rms(d(merged.astype(BF16), wo_ref[...]), n1_ref[...])
    h2 = _rms(x1, n2pre_ref[...]).astype(BF16)
    u = jnp.square(jnp.maximum(d(h2, wu_ref[...]), 0.0))
    out_ref[...] = x1 + _rms(d(u.astype(BF16), wd_ref[...]), n2_ref[...])


def _merge_ffn(x, y_att, y_rw, gates, wa, wr, wo, wu, wd, n1_post, n2_pre, n2_post):
    rows = x.shape[0]
    tm = 256
    row = lambda w: pl.BlockSpec((tm, w), lambda i: (i, 0))
    const = lambda a: pl.BlockSpec(a.shape, lambda i: (0, 0), pipeline_mode=pl.Buffered(1))
    consts = [wa, wr, wo, wu, wd, n1_post, n2_pre, n2_post]
    return pl.pallas_call(
        _merge_ffn_kernel,
        grid=(rows // tm,),
        in_specs=[row(D_MODEL), row(ATT_W), row(RW_W), row(GATE_W)] + [const(a) for a in consts],
        out_specs=row(D_MODEL),
        out_shape=jax.ShapeDtypeStruct((rows, D_MODEL), F32),
        compiler_params=pltpu.CompilerParams(dimension_semantics=("arbitrary",), vmem_limit_bytes=VMEM_LIMIT),
        name="merge_ffn",
    )(x, y_att, y_rw, gates, *consts)


PAGES_PER_BLOCK = MOBA_BLOCK // PAGE_SIZE
WAVE = 16
SELECT_SLOTS = 3
ATTEND_SLOTS = 4
SEL_BITS = 5


def _sample_select_kernel(pt_ref, qt_ref, ck_ref, sel_ref, buf, sem, sc_ref, *, n_pages, n_batch, dec_seq):
    b = pl.program_id(0)
    waves_per_b = n_pages // WAVE
    n_waves = n_batch * waves_per_b
    n_full = n_pages // PAGES_PER_BLOCK

    def copies(g):
        bb, w, slot = g // waves_per_b, g % waves_per_b, g % SELECT_SLOTS
        return [pltpu.make_async_copy(ck_ref.at[pt_ref[bb, w * WAVE + i]], buf.at[slot, i], sem.at[slot])
                for i in range(WAVE)]

    @pl.when(b == 0)
    def _():
        for g in range(SELECT_SLOTS - 1):
            for cp in copies(g):
                cp.start()

    qt = qt_ref[0]
    lane = lax.broadcasted_iota(jnp.int32, (HEAD_DIM, LANES), 1)
    for w in range(waves_per_b):
        g = b * waves_per_b + w
        slot = g % SELECT_SLOTS

        @pl.when(g + SELECT_SLOTS - 1 < n_waves)
        def _():
            for cp in copies(g + SELECT_SLOTS - 1):
                cp.start()

        for cp in copies(g):
            cp.wait()
        for i in range(0, WAVE, PAGES_PER_BLOCK):
            msel = jnp.zeros((HEAD_DIM, LANES), F32)
            for hh in range(N_HEADS):
                t = buf[slot, i, hh]
                for j in range(1, PAGES_PER_BLOCK):
                    t = t + buf[slot, i + j, hh]
                ksum = jnp.sum(t, axis=1, keepdims=True)
                msel = jnp.where(lane // dec_seq == hh, ksum, msel)
            n = (w * WAVE + i) // PAGES_PER_BLOCK
            sc_ref[n:n + 1, :] = jnp.sum(msel * qt, axis=0, keepdims=True) * (1.0 / MOBA_BLOCK)

    sc = sc_ref[...]
    blk = lax.broadcasted_iota(jnp.int32, (n_full, LANES), 0)
    row = lax.broadcasted_iota(jnp.int32, (8, LANES), 0)
    out = jnp.zeros((8, LANES), jnp.int32)
    for r in range(MOBA_TOPK):
        m = jnp.max(sc, axis=0, keepdims=True)
        idx = jnp.min(jnp.where(sc == m, blk, n_full), axis=0, keepdims=True)
        out = jnp.where(row == r, idx, out)
        sc = jnp.where(blk == idx, -jnp.inf, sc)
    sel_ref[0] = out


def _sample_select(page_table, q_t, ck_view, dec_seq):
    n_batch, n_pages = page_table.shape
    n_full = n_pages // PAGES_PER_BLOCK
    return pl.pallas_call(
        functools.partial(_sample_select_kernel, n_pages=n_pages, n_batch=n_batch, dec_seq=dec_seq),
        grid_spec=pltpu.PrefetchScalarGridSpec(
            num_scalar_prefetch=1,
            grid=(n_batch,),
            in_specs=[pl.BlockSpec((1, HEAD_DIM, LANES), lambda b, pt: (b, 0, 0)),
                      pl.BlockSpec(memory_space=pl.ANY)],
            out_specs=pl.BlockSpec((1, 8, LANES), lambda b, pt: (b, 0, 0)),
            scratch_shapes=[pltpu.VMEM((SELECT_SLOTS, WAVE, N_HEADS, HEAD_DIM, PAGE_SIZE), F32),
                            pltpu.SemaphoreType.DMA((SELECT_SLOTS,)),
                            pltpu.VMEM((n_full, LANES), F32)],
        ),
        out_shape=jax.ShapeDtypeStruct((n_batch, 8, LANES), jnp.int32),
        compiler_params=pltpu.CompilerParams(dimension_semantics=("arbitrary",), vmem_limit_bytes=VMEM_LIMIT),
        name="sample_select",
    )(page_table, q_t, ck_view)


def _sample_attend_kernel(pt_ref, sel_ref, wq_ref, kn_ref, vn_ref, ck_ref, cv_ref, o_ref, kbuf, vbuf, sem,
                          *, n_batch, dec_seq):
    b = pl.program_id(0)
    h = pl.program_id(1)
    seg = MOBA_TOPK * MOBA_BLOCK

    def copies(g):
        bb, hh, slot = g // N_HEADS, g % N_HEADS, g % ATTEND_SLOTS
        cps = []
        for qi in range(dec_seq):
            packed = sel_ref[bb, hh * dec_seq + qi]
            for r in range(MOBA_TOPK):
                blk = (packed >> (SEL_BITS * r)) & ((1 << SEL_BITS) - 1)
                for pg in range(PAGES_PER_BLOCK):
                    page = pt_ref[bb, blk * PAGES_PER_BLOCK + pg]
                    t = (qi * MOBA_TOPK + r) * PAGES_PER_BLOCK + pg
                    dst = pl.ds(t * PAGE_SIZE, PAGE_SIZE)
                    cps.append(pltpu.make_async_copy(ck_ref.at[page, hh], kbuf.at[slot, :, dst], sem.at[slot, 0]))
                    cps.append(pltpu.make_async_copy(cv_ref.at[page, hh], vbuf.at[slot, :, dst], sem.at[slot, 1]))
        return cps

    g = b * N_HEADS + h
    slot = g % ATTEND_SLOTS

    @pl.when(g == 0)
    def _():
        for g0 in range(ATTEND_SLOTS - 1):
            for cp in copies(g0):
                cp.start()

    @pl.when(g + ATTEND_SLOTS - 1 < n_batch * N_HEADS)
    def _():
        for cp in copies(g + ATTEND_SLOTS - 1):
            cp.start()

    for cp in copies(g):
        cp.wait()

    wq = wq_ref[0]
    kn = kn_ref[0]
    vn = vn_ref[0]
    lane_d = lax.broadcasted_iota(jnp.int32, (HEAD_DIM, LANES), 1)
    lane_1 = lax.broadcasted_iota(jnp.int32, (1, LANES), 1)
    @pl.when(h == 0)
    def _():
        o_ref[...] = jnp.zeros_like(o_ref)

    out = o_ref[0]
    for qi in range(dec_seq):
        me = h * dec_seq + qi
        qcol = jnp.sum(jnp.where(lane_d == me, wq, 0.0), axis=1, keepdims=True)
        keys = slice(qi * seg, (qi + 1) * seg)
        s = jnp.sum(kbuf[slot, :, keys] * qcol, axis=0, keepdims=True)
        s_own = jnp.sum(kn * qcol, axis=0, keepdims=True)
        own = jnp.logical_and(lane_1 // dec_seq == h, lane_1 % dec_seq <= qi)
        s_own = jnp.where(own, s_own, NEG_INF)
        m = jnp.maximum(jnp.max(s, axis=1, keepdims=True), jnp.max(s_own, axis=1, keepdims=True))
        p = jnp.exp(s - m)
        p_own = jnp.exp(s_own - m)
        l = jnp.sum(p, axis=1, keepdims=True) + jnp.sum(p_own, axis=1, keepdims=True)
        o = jnp.sum(vbuf[slot, :, keys] * p, axis=1, keepdims=True) + jnp.sum(vn * p_own, axis=1, keepdims=True)
        out = jnp.where(lane_d == me, o * (1.0 / l), out)
    o_ref[0] = out


def _sample_attend(page_table, sel_packed, wq, kn_t, vn_t, ck_view, cv_view, dec_seq):
    n_batch = page_table.shape[0]
    n_keys = dec_seq * MOBA_TOPK * MOBA_BLOCK
    nblk = pl.BlockSpec((1, HEAD_DIM, LANES), lambda b, h, pt, sel: (b, 0, 0))
    return pl.pallas_call(
        functools.partial(_sample_attend_kernel, n_batch=n_batch, dec_seq=dec_seq),
        grid_spec=pltpu.PrefetchScalarGridSpec(
            num_scalar_prefetch=2,
            grid=(n_batch, N_HEADS),
            in_specs=[nblk, nblk, nblk, pl.BlockSpec(memory_space=pl.ANY), pl.BlockSpec(memory_space=pl.ANY)],
            out_specs=nblk,
            scratch_shapes=[pltpu.VMEM((ATTEND_SLOTS, HEAD_DIM, n_keys), F32),
                            pltpu.VMEM((ATTEND_SLOTS, HEAD_DIM, n_keys), F32),
                            pltpu.SemaphoreType.DMA((ATTEND_SLOTS, 2))],
        ),
        out_shape=jax.ShapeDtypeStruct((n_batch, HEAD_DIM, LANES), F32),
        compiler_params=pltpu.CompilerParams(dimension_semantics=("arbitrary", "arbitrary"),
                                             vmem_limit_bytes=VMEM_LIMIT),
        name="sample_attend",
    )(page_table, sel_packed, wq, kn_t, vn_t, ck_view, cv_view)


def kernel(x_prompt, x_sample, cache_k, cache_v, state_wkv, state_shift, page_table, n1_pre, n1_post, n2_pre, n2_post, w_in, rw_mu, rw_w0, rw_w2, rw_a0, rw_a2, rw_g2, rw_kk, rw_ka, rw_rk, rw_ln_w, rw_ln_b, w_att_br, w_rw_br, w_out, w_up, w_down):
    _, seq, _ = x_prompt.shape
    n_batch, dec_seq, _ = x_sample.shape
    n_pages = page_table.shape[1]
    past = n_pages * PAGE_SIZE
    assert past % MOBA_BLOCK == 0 and n_pages // PAGES_PER_BLOCK >= MOBA_TOPK
    assert n_pages // PAGES_PER_BLOCK <= (1 << SEL_BITS) and x_prompt.shape[0] == 1 and n1_pre.shape[0] == 1
    l = 0
    row2 = lambda a: a[l].reshape(1, -1)
    rwp = dict(rw_mu=row2(rw_mu), rw_w0=row2(rw_w0), rw_w2=rw_w2[l], rw_a0=row2(rw_a0), rw_a2=rw_a2[l],
               rw_g2=rw_g2[l], rw_kk=row2(rw_kk), rw_ka=row2(rw_ka), rw_rk=row2(rw_rk),
               rw_ln_w=row2(rw_ln_w), rw_ln_b=row2(rw_ln_b))
    w_in_b = w_in[l].astype(BF16)
    ffn_w = [w.astype(BF16) for w in (w_att_br[l], w_rw_br[l], w_out[l], w_up[l], w_down[l])]
    norms = [row2(n1_post), row2(n2_pre), row2(n2_post)]
    heads_t = lambda t: t.reshape(N_HEADS, HEAD_DIM, -1)

    xp = x_prompt[0]
    cos, sin = _rope_tables(jnp.arange(seq))
    q, kb, vtb, kt, vt, rw, gate, kmean = _project(xp, cos, sin, row2(n1_pre), w_in_b)
    n_blk = seq // MOBA_BLOCK
    y_att = _moba_prompt(q, kb.reshape(n_blk, MOBA_BLOCK, ATT_W), vtb, kmean.reshape(n_blk, ATT_W))
    y_rw, wkv_p = _rwkv_prompt(rw, rwp)
    y_prompt = _merge_ffn(xp, y_att, y_rw, gate, *ffn_w, *norms)
    k_prompt = heads_t(kt).transpose(0, 2, 1)[None, None]
    v_prompt = heads_t(vt).transpose(0, 2, 1)[None, None]

    xs = x_sample.reshape(n_batch * dec_seq, D_MODEL)
    cos_s, sin_s = _rope_tables(jnp.tile(past + jnp.arange(dec_seq), n_batch))
    qs, _, _, kt_s, vt_s, rw_s, gate_s, _ = _project(xs, cos_s, sin_s, row2(n1_pre), w_in_b)
    kn_t = heads_t(kt_s).reshape(N_HEADS, HEAD_DIM, n_batch, dec_seq)
    vn_t = heads_t(vt_s).reshape(N_HEADS, HEAD_DIM, n_batch, dec_seq)
    q4 = qs.reshape(n_batch, dec_seq, N_HEADS, HEAD_DIM)
    q_t = q4.transpose(0, 3, 2, 1).reshape(n_batch, HEAD_DIM, N_HEADS * dec_seq)
    q_t = jnp.pad(q_t, ((0, 0), (0, 0), (0, LANES - N_HEADS * dec_seq)))
    ck_view = jnp.swapaxes(cache_k[:, l], -1, -2)
    cv_view = jnp.swapaxes(cache_v[:, l], -1, -2)
    sel = _sample_select(page_table, q_t, ck_view, dec_seq)
    sel = sel[:, :MOBA_TOPK, :N_HEADS * dec_seq]
    sel_packed = sel[:, 0] | (sel[:, 1] << SEL_BITS) | (sel[:, 2] << (2 * SEL_BITS))
    pad_new = lambda t: jnp.pad(t.transpose(2, 1, 0, 3).reshape(n_batch, HEAD_DIM, N_HEADS * dec_seq),
                                ((0, 0), (0, 0), (0, LANES - N_HEADS * dec_seq)))
    o_s = _sample_attend(page_table, sel_packed, q_t * (HEAD_DIM ** -0.5), pad_new(kn_t), pad_new(vn_t),
                         ck_view, cv_view, dec_seq)
    o_s = o_s[:, :, :N_HEADS * dec_seq].reshape(n_batch, HEAD_DIM, N_HEADS, dec_seq)
    y_att_s = o_s.transpose(0, 3, 2, 1).reshape(n_batch * dec_seq, ATT_W).astype(BF16)

    rw_s3 = rw_s.reshape(n_batch, dec_seq, SHIFT_W)
    prev_s = jnp.concatenate([state_shift[:, l][:, None], rw_s3[:, :-1]], axis=1).reshape(n_batch * dec_seq, SHIFT_W)
    to_steps = lambda t: t.reshape(n_batch, dec_seq, RW_HEADS, RW_N).transpose(1, 2, 3, 0)
    vecs = [to_steps(t) for t in _rwkv_mix_call(rw_s, prev_s, rwp)]
    on_lanes = lambda a: jnp.broadcast_to(a[l].reshape(RW_HEADS, RW_N, 1), (RW_HEADS, RW_N, n_batch))
    y_steps, wkv_s = _rwkv_steps(vecs, state_wkv[:, l].transpose(1, 2, 3, 0),
                                 on_lanes(rw_ln_w), on_lanes(rw_ln_b), on_lanes(rw_rk))
    y_rw_s = y_steps.transpose(3, 0, 1, 2).reshape(n_batch * dec_seq, RW_W).astype(BF16)
    y_sample = _merge_ffn(xs, y_att_s, y_rw_s, gate_s, *ffn_w, *norms)
    k_sample = kn_t.transpose(2, 0, 3, 1)[:, None]
    v_sample = vn_t.transpose(2, 0, 3, 1)[:, None]

    return (y_prompt[None], y_sample.reshape(n_batch, dec_seq, D_MODEL), k_prompt, v_prompt, k_sample, v_sample,
            wkv_p[None, None], wkv_s.transpose(3, 0, 1, 2)[:, None], rw[-1][None, None], rw_s3[:, -1][:, None])
```

```python
import functools
import math

import jax
import jax.numpy as jnp
from jax import lax
from jax.experimental import pallas as pl
from jax.experimental.pallas import tpu as pltpu

F32 = jnp.float32
BF16 = jnp.bfloat16

D_MODEL = 1024
N_HEADS = 8
HEAD_DIM = 64
ATT_W = N_HEADS * HEAD_DIM
RW_HEADS = 8
RW_N = 64
RW_W = RW_HEADS * RW_N
DECAY_LORA = 64
AAA_LORA = 64
GATE_LORA = 128
D_FF = 4 * D_MODEL
MOBA_BLOCK = 256
MOBA_TOPK = 3
Q_BLOCK = 128
PAGE_SIZE = 128
ROPE_THETA = 10000.0
NORM_EPS = 1e-6
GN_EPS = 64e-5
NEG_INF = -1e30
LOG2_E = 1.4426950408889634
SHIFT_W = 3 * RW_W + DECAY_LORA + AAA_LORA + GATE_LORA
GATE_W = 2 * D_MODEL
IN_W = 3 * ATT_W + SHIFT_W + GATE_W

LANES = 128
SUBLANES = 8
ONES_ROWS = 16
MOBA_GROUP = 4
MOBA_COLS = (MOBA_BLOCK // Q_BLOCK) * 2 * Q_BLOCK
VMEM_LIMIT = 56 * 1024 * 1024

NT_DIMS = (((1,), (1,)), ((), ()))
TN_DIMS = (((0,), (0,)), ((), ()))


def _split2(x):
    hi = x.astype(BF16)
    lo = (x - hi.astype(F32)).astype(BF16)
    return hi, lo


def _dot3(a, b, dims=(((1,), (0,)), ((), ()))):
    ah, al = _split2(a)
    bh, bl = _split2(b)
    d = functools.partial(lax.dot_general, dimension_numbers=dims, preferred_element_type=F32)
    return d(ah, bh) + (d(ah, bl) + d(al, bh))


def _split3(x):
    x1 = x.astype(BF16)
    r1 = x - x1.astype(F32)
    x2 = r1.astype(BF16)
    x3 = (r1 - x2.astype(F32)).astype(BF16)
    return x1, x2, x3


def _dot_exact_lhs(mask, b):
    a = mask.astype(BF16)
    b1, b2, b3 = _split3(b)
    d = functools.partial(jnp.dot, preferred_element_type=F32)
    return d(a, b1) + (d(a, b2) + d(a, b3))


def _head_sums(x):
    li = lax.broadcasted_iota(jnp.int32, (LANES, LANES), 0) // RW_N
    lj = lax.broadcasted_iota(jnp.int32, (LANES, LANES), 1) // RW_N
    ones = (li == lj).astype(BF16)
    d = functools.partial(jnp.dot, preferred_element_type=F32)
    out = []
    for c in range(RW_W // LANES):
        hi, lo = _split2(x[:, c * LANES:(c + 1) * LANES])
        out.append(d(hi, ones) + d(lo, ones))
    return jnp.concatenate(out, axis=1)


def _sigmoid(x):
    return 1.0 / (1.0 + jnp.exp(-x))


def _proj_kernel(x_ref, cos_ref, sin_ref, g_ref, w_ref,
                 q_ref, kb_ref, vtb_ref, kt_ref, vt_ref, rw_ref, gate_ref, kmean_ref, *, tm):
    x = x_ref[...]
    ms = jnp.mean(x * x, axis=-1, keepdims=True)
    h = (x * lax.rsqrt(ms + NORM_EPS) * g_ref[...]).astype(BF16)
    cos = cos_ref[...]
    sin = sin_ref[...]
    lane = lax.broadcasted_iota(jnp.int32, (tm, LANES), 1)
    first_half = (lane % HEAD_DIM) < (HEAD_DIM // 2)

    def rope(p):
        sw = jnp.where(first_half, pltpu.roll(p, LANES - HEAD_DIM // 2, 1), pltpu.roll(p, HEAD_DIM // 2, 1))
        return p * cos + sw * sin

    def proj(lo, width):
        return jnp.dot(h, w_ref[:, lo:lo + width], preferred_element_type=F32)

    pq = proj(0, ATT_W)
    for c in range(ATT_W // LANES):
        q_ref[:, c * LANES:(c + 1) * LANES] = rope(pq[:, c * LANES:(c + 1) * LANES])
    pk = proj(ATT_W, ATT_W)
    for c in range(ATT_W // LANES):
        cols = slice(c * LANES, (c + 1) * LANES)
        kr = rope(pk[:, cols])
        kb_ref[:, cols] = kr.astype(BF16)
        kt_ref[cols, :] = kr.T
        kmean_ref[0, :, cols] = jnp.sum(kr, axis=0, keepdims=True) * (1.0 / tm)
    pv = proj(2 * ATT_W, ATT_W)
    for c in range(ATT_W // LANES):
        cols = slice(c * LANES, (c + 1) * LANES)
        vt = pv[:, cols].T
        vt_ref[cols, :] = vt
        vtb_ref[0, cols, :] = vt.astype(BF16)
    rw_ref[...] = proj(3 * ATT_W, SHIFT_W)
    gate_ref[...] = _sigmoid(proj(3 * ATT_W + SHIFT_W, GATE_W))


def _rope_tables(pos):
    half = HEAD_DIM // 2
    inv = ROPE_THETA ** (-jnp.arange(half, dtype=F32) / half)
    ang = pos.astype(F32)[:, None] * inv[None, :]
    cos = jnp.cos(ang)
    sin = jnp.sin(ang)
    return jnp.tile(jnp.concatenate([cos, cos], axis=1), (1, 2)), jnp.tile(jnp.concatenate([-sin, sin], axis=1), (1, 2))


def _project(x, cos, sin, g, w_in_bf16):
    rows = x.shape[0]
    tm = MOBA_BLOCK
    n = rows // tm
    row = lambda w: pl.BlockSpec((tm, w), lambda i: (i, 0))
    const = lambda a, b: pl.BlockSpec((a, b), lambda i: (0, 0))
    col = pl.BlockSpec((ATT_W, tm), lambda i: (0, i))
    return pl.pallas_call(
        functools.partial(_proj_kernel, tm=tm),
        grid=(n,),
        in_specs=[row(D_MODEL), row(LANES), row(LANES), const(1, D_MODEL), const(D_MODEL, IN_W)],
        out_specs=[row(ATT_W), row(ATT_W), pl.BlockSpec((1, ATT_W, tm), lambda i: (i, 0, 0)), col, col,
                   row(SHIFT_W), row(GATE_W), pl.BlockSpec((1, 1, ATT_W), lambda i: (i, 0, 0))],
        out_shape=[
            jax.ShapeDtypeStruct((rows, ATT_W), F32),
            jax.ShapeDtypeStruct((rows, ATT_W), BF16),
            jax.ShapeDtypeStruct((n, ATT_W, tm), BF16),
            jax.ShapeDtypeStruct((ATT_W, rows), F32),
            jax.ShapeDtypeStruct((ATT_W, rows), F32),
            jax.ShapeDtypeStruct((rows, SHIFT_W), F32),
            jax.ShapeDtypeStruct((rows, GATE_W), F32),
            jax.ShapeDtypeStruct((n, 1, ATT_W), F32),
        ],
        compiler_params=pltpu.CompilerParams(dimension_semantics=("arbitrary",), vmem_limit_bytes=VMEM_LIMIT),
        name="project",
    )(x, cos, sin, g, w_in_bf16)


def _moba_prompt_kernel(q_ref, k_ref, vt_ref, kmean_ref, y_ref, sel_ref, sa_ref, sb_ref, *, n_blk):
    ob = pl.program_id(1)
    lane = lax.broadcasted_iota(jnp.int32, (Q_BLOCK, LANES), 1)
    kmean = kmean_ref[...]
    blk = lax.broadcasted_iota(jnp.int32, (n_blk, Q_BLOCK), 0)
    qs = []
    for ch in range(MOBA_BLOCK // Q_BLOCK):
        q = q_ref[ch * Q_BLOCK:(ch + 1) * Q_BLOCK, :]
        for hh in range(2):
            qh = jnp.where((lane // HEAD_DIM) == hh, q, 0.0)
            qs.append((qh * (HEAD_DIM ** -0.5 * LOG2_E)).astype(BF16))
            sb = _dot3(kmean, qh, NT_DIMS)
            sb = jnp.where(blk < ob, sb, NEG_INF)
            sel = jnp.zeros((n_blk, Q_BLOCK), jnp.bool_)
            for _ in range(MOBA_TOPK):
                m = jnp.max(sb, axis=0, keepdims=True)
                idx = jnp.min(jnp.where(sb == m, blk, n_blk), axis=0, keepdims=True)
                pick = blk == idx
                sel = jnp.logical_or(sel, pick)
                sb = jnp.where(pick, -jnp.inf, sb)
            sel = jnp.logical_and(sel, blk < ob)
            col0 = (2 * ch + hh) * Q_BLOCK
            sel_ref[:, col0:col0 + Q_BLOCK] = sel.astype(F32)
    qs = jnp.concatenate(qs, axis=0)

    def scores(j):
        return lax.dot_general(k_ref[j], qs, NT_DIMS, preferred_element_type=F32)

    s = scores(ob)
    key = lax.broadcasted_iota(jnp.int32, (MOBA_BLOCK, MOBA_COLS), 0)
    col = lax.broadcasted_iota(jnp.int32, (MOBA_BLOCK, MOBA_COLS), 1)
    s = jnp.where(key <= (col // (2 * Q_BLOCK)) * Q_BLOCK + col % Q_BLOCK, s, NEG_INF)
    m0 = jnp.max(s, axis=0, keepdims=True)
    ones_rows = jnp.ones((ONES_ROWS, MOBA_BLOCK), BF16)

    def pv(j, p):
        return jnp.dot(jnp.concatenate([vt_ref[j], ones_rows], axis=0), p.astype(BF16), preferred_element_type=F32)

    acc0 = pv(ob, jnp.exp2(s - m0))

    def block(g, t):
        return jnp.minimum(MOBA_GROUP * g + t, n_blk - 1)

    def masked_scores(j):
        return jnp.where(sel_ref[pl.ds(j, 1), :] > 0.0, scores(j), NEG_INF)

    def group_step(g, cur_ref, nxt_ref, m_cur, m_acc, acc):
        mx = jnp.full((1, MOBA_COLS), NEG_INF, F32)
        tot = None
        for t in range(MOBA_GROUP):
            sn = masked_scores(block(g + 1, t))
            nxt_ref[t] = sn
            mx = jnp.maximum(mx, jnp.max(sn, axis=0, keepdims=True))
            part = pv(block(g, t), jnp.exp2(cur_ref[t] - m_cur))
            tot = part if tot is None else tot + part
        return jnp.maximum(m_cur, mx), m_cur, acc * jnp.exp2(m_acc - m_cur) + tot

    mx = m0
    for t in range(MOBA_GROUP):
        s = masked_scores(block(0, t))
        sa_ref[t] = s
        mx = jnp.maximum(mx, jnp.max(s, axis=0, keepdims=True))

    def body(i, carry):
        m_cur, m_acc, acc = group_step(2 * i, sa_ref, sb_ref, *carry)
        return group_step(2 * i + 1, sb_ref, sa_ref, m_cur, m_acc, acc)

    n_groups = (ob + MOBA_GROUP - 1) // MOBA_GROUP
    _, _, acc = lax.fori_loop(0, (n_groups + 1) // 2, body, (mx, m0, acc0))
    o = acc[:LANES] * (1.0 / acc[LANES:LANES + 1])
    row = lax.broadcasted_iota(jnp.int32, (LANES, Q_BLOCK), 0)
    for ch in range(MOBA_BLOCK // Q_BLOCK):
        c0 = 2 * ch * Q_BLOCK
        ot = jnp.where(row < HEAD_DIM, o[:, c0:c0 + Q_BLOCK], o[:, c0 + Q_BLOCK:c0 + 2 * Q_BLOCK])
        y_ref[ch * Q_BLOCK:(ch + 1) * Q_BLOCK, :] = ot.T.astype(y_ref.dtype)


def _moba_prompt(q, k_blocks, vt_blocks, kmean):
    s = q.shape[0]
    n_blk = k_blocks.shape[0]
    return pl.pallas_call(
        functools.partial(_moba_prompt_kernel, n_blk=n_blk),
        grid=(ATT_W // LANES, n_blk),
        in_specs=[
            pl.BlockSpec((MOBA_BLOCK, LANES), lambda hp, c: (c, hp)),
            pl.BlockSpec((n_blk, MOBA_BLOCK, LANES), lambda hp, c: (0, 0, hp)),
            pl.BlockSpec((n_blk, LANES, MOBA_BLOCK), lambda hp, c: (0, hp, 0)),
            pl.BlockSpec((n_blk, LANES), lambda hp, c: (0, hp)),
        ],
        out_specs=pl.BlockSpec((MOBA_BLOCK, LANES), lambda hp, c: (c, hp)),
        out_shape=jax.ShapeDtypeStruct((s, ATT_W), BF16),
        scratch_shapes=[pltpu.VMEM((n_blk, MOBA_COLS), F32),
                        pltpu.VMEM((MOBA_GROUP, MOBA_BLOCK, MOBA_COLS), F32),
                        pltpu.VMEM((MOBA_GROUP, MOBA_BLOCK, MOBA_COLS), F32)],
        compiler_params=pltpu.CompilerParams(dimension_semantics=("arbitrary", "arbitrary"),
                                             vmem_limit_bytes=VMEM_LIMIT),
        name="moba_prompt",
    )(q, k_blocks, vt_blocks, kmean)


def _rwkv_mix(rw, prev, mu_ref, w0_ref, w2_ref, a0_ref, a2_ref, g2_ref, kk_ref, ka_ref):
    m = rw + (prev - rw) * mu_ref[...]
    r = m[:, 0:RW_W]
    k = m[:, RW_W:2 * RW_W]
    v = m[:, 2 * RW_W:3 * RW_W]
    o1 = 3 * RW_W
    wd = m[:, o1:o1 + DECAY_LORA]
    ad = m[:, o1 + DECAY_LORA:o1 + DECAY_LORA + AAA_LORA]
    gd = m[:, o1 + DECAY_LORA + AAA_LORA:]
    bdot = lambda a, b: jnp.dot(a.astype(BF16), b.astype(BF16), preferred_element_type=F32)
    w_raw = w0_ref[...] + bdot(jnp.tanh(wd), w2_ref[...])
    logw = -math.exp(-0.5) * _sigmoid(w_raw)
    a_sig = _sigmoid(a0_ref[...] + bdot(ad, a2_ref[...]))
    g = bdot(_sigmoid(gd), g2_ref[...])

    kk = k * kk_ref[...]
    kk_norm = jnp.sqrt(_head_sums(kk * kk))
    kk = kk / jnp.maximum(kk_norm, 1e-12)
    k2 = k * (1.0 + (a_sig - 1.0) * ka_ref[...])
    return r, logw, k2, v, -kk, kk * a_sig, g


def _rwkv_kernel(pt_ref, rw_ref, tail_ref, mu_ref, w0_ref, w2_ref, a0_ref, a2_ref, g2_ref, kk_ref, ka_ref,
                 rk_ref, lnw_ref, lnb_ref, qt_ref, ck_ref, y_ref, s_ref, sel_ref, buf, sem, sc_ref,
                 *, chunk, select):
    @pl.when(pl.program_id(0) == 0)
    def _():
        s_ref[...] = jnp.zeros_like(s_ref)

    _select_step(pl.program_id(0), pt_ref, qt_ref, ck_ref, buf, sem, sc_ref, **select)

    rw = rw_ref[...]
    before = jnp.where(pl.program_id(0) == 0, 0.0, tail_ref[SUBLANES - 1:SUBLANES, :])
    first_row = lax.broadcasted_iota(jnp.int32, rw.shape, 0) == 0
    prev = jnp.where(first_row, before, pltpu.roll(rw, 1, 0))
    r, logw, k2, v, a_vec, b_vec, g = _rwkv_mix(rw, prev, mu_ref, w0_ref, w2_ref, a0_ref, a2_ref,
                                                g2_ref, kk_ref, ka_ref)
    ti = lax.broadcasted_iota(jnp.int32, (chunk, chunk), 0)
    tj = lax.broadcasted_iota(jnp.int32, (chunk, chunk), 1)
    lower_incl = tj <= ti
    lower_strict = tj < ti
    cs = _dot_exact_lhs(lower_incl.astype(F32), logw)
    p_incl = jnp.exp(cs)
    p_inv = jnp.exp(-cs)
    a_t = a_vec * jnp.exp(cs - logw)
    r_t = r * p_incl
    b_t = b_vec * p_inv
    k_t = k2 * p_inv
    p_last = p_incl[chunk - 1:chunk, :]
    eye = (ti == tj).astype(F32)

    heads = range(RW_HEADS)
    sl = [slice(h * RW_N, (h + 1) * RW_N) for h in heads]
    ar = [jnp.concatenate([a_t[:, s], r_t[:, s]], axis=0) for s in sl]
    bk = [jnp.concatenate([b_t[:, s], k_t[:, s]], axis=0) for s in sl]
    gram = [_dot3(ar[h], bk[h], NT_DIMS) for h in heads]
    s0 = [s_ref[h] for h in heads]
    ars = [_dot3(ar[h], s0[h], NT_DIMS) for h in heads]
    l_ak_v = [_dot3(jnp.where(lower_strict, gram[h][:chunk, chunk:], 0.0), v[:, sl[h]]) for h in heads]
    l_ab = [jnp.where(lower_strict, gram[h][:chunk, :chunk], 0.0) for h in heads]
    x = [eye + l_ab[h] for h in heads]
    pw = [_dot3(l_ab[h], l_ab[h]) for h in heads]
    n_factors = int(math.log2(chunk))
    for it in range(1, n_factors):
        if it + 1 < n_factors:
            prod = [_dot3(pw[h], jnp.concatenate([pw[h], x[h]], axis=1)) for h in heads]
            pw = [prod[h][:, :chunk] for h in heads]
            x = [x[h] + prod[h][:, chunk:] for h in heads]
        else:
            x = [x[h] + _dot3(pw[h], x[h]) for h in heads]
    u = [_dot3(x[h], ars[h][:chunk] + l_ak_v[h]) for h in heads]
    uv = [jnp.concatenate([u[h], v[:, sl[h]]], axis=0) for h in heads]
    wide_i = lax.broadcasted_iota(jnp.int32, (chunk, 2 * chunk), 0)
    wide_j = lax.broadcasted_iota(jnp.int32, (chunk, 2 * chunk), 1) % chunk
    m_r = [jnp.where(wide_j <= wide_i, gram[h][chunk:, :], 0.0) for h in heads]
    ys = [ars[h][chunk:] + _dot3(m_r[h], uv[h]) for h in heads]
    for h in heads:
        s_ref[h] = (s0[h] + _dot3(uv[h], bk[h], TN_DIMS)) * p_last[:, sl[h]]
    o = jnp.concatenate(ys, axis=1)

    mean = _head_sums(o) * (1.0 / RW_N)
    d = o - mean
    var = _head_sums(d * d) * (1.0 / RW_N)
    o = d * lax.rsqrt(var + GN_EPS) * lnw_ref[...] + lnb_ref[...]
    bonus = _head_sums(r * k2 * rk_ref[...]) * v
    y_ref[...] = ((o + bonus) * g).astype(y_ref.dtype)
    _select_finish(pl.program_id(0), sel_ref, sc_ref, **select)


RW_CHUNK = 64

_MIX_PARAMS = ('rw_mu', 'rw_w0', 'rw_w2', 'rw_a0', 'rw_a2', 'rw_g2', 'rw_kk', 'rw_ka')


def _rwkv_prompt_and_select(rw, p, page_table, q_t, ck_view, dec_seq):
    rows = rw.shape[0]
    n_steps = rows // RW_CHUNK
    n_batch, n_pages = page_table.shape
    n_waves = n_batch * (n_pages // WAVE)
    waves_per_step, rem = divmod(n_waves, n_steps)
    assert rem == 0 and waves_per_step >= 1
    steps_per_b = n_steps // n_batch
    rowspec = lambda w: pl.BlockSpec((RW_CHUNK, w), lambda c, pt: (c, 0))
    tail = pl.BlockSpec((SUBLANES, SHIFT_W), lambda c, pt: (jnp.maximum(c * (RW_CHUNK // SUBLANES) - 1, 0), 0))
    const = lambda a: pl.BlockSpec(a.shape, lambda c, pt: (0,) * a.ndim)
    params = [p[n] for n in _MIX_PARAMS] + [p['rw_rk'], p['rw_ln_w'], p['rw_ln_b']]
    select = dict(n_pages=n_pages, n_batch=n_batch, dec_seq=dec_seq, waves_per_step=waves_per_step)
    return pl.pallas_call(
        functools.partial(_rwkv_kernel, chunk=RW_CHUNK, select=select),
        grid_spec=pltpu.PrefetchScalarGridSpec(
            num_scalar_prefetch=1,
            grid=(n_steps,),
            in_specs=[rowspec(SHIFT_W), tail] + [const(a) for a in params]
                     + [pl.BlockSpec((1, HEAD_DIM, LANES), lambda c, pt: (c // steps_per_b, 0, 0)),
                        pl.BlockSpec(memory_space=pl.ANY)],
            out_specs=[rowspec(RW_W), pl.BlockSpec((RW_HEADS, RW_N, RW_N), lambda c, pt: (0, 0, 0)),
                       pl.BlockSpec((1, 8, LANES), lambda c, pt: (c // steps_per_b, 0, 0))],
            scratch_shapes=[pltpu.VMEM((SELECT_SLOTS, WAVE, N_HEADS, HEAD_DIM, PAGE_SIZE), F32),
                            pltpu.SemaphoreType.DMA((SELECT_SLOTS,)),
                            pltpu.VMEM((n_pages // PAGES_PER_BLOCK, LANES), F32)],
        ),
        out_shape=[jax.ShapeDtypeStruct((rows, RW_W), BF16),
                   jax.ShapeDtypeStruct((RW_HEADS, RW_N, RW_N), F32),
                   jax.ShapeDtypeStruct((n_batch, 8, LANES), jnp.int32)],
        compiler_params=pltpu.CompilerParams(dimension_semantics=("arbitrary",), vmem_limit_bytes=VMEM_LIMIT),
        name="rwkv7_prompt_select",
    )(page_table, rw, rw, *params, q_t, ck_view)


def _rwkv_mix_kernel(rw_ref, prev_ref, mu_ref, w0_ref, w2_ref, a0_ref, a2_ref, g2_ref, kk_ref, ka_ref,
                     r_ref, w_ref, k_ref, v_ref, a_ref, b_ref, g_ref):
    r, logw, k2, v, a_vec, b_vec, g = _rwkv_mix(rw_ref[...], prev_ref[...], mu_ref, w0_ref, w2_ref, a0_ref, a2_ref,
                                                g2_ref, kk_ref, ka_ref)
    r_ref[...] = r
    w_ref[...] = jnp.exp(logw)
    k_ref[...] = k2
    v_ref[...] = v
    a_ref[...] = a_vec
    b_ref[...] = b_vec
    g_ref[...] = g


def _rwkv_mix_call(rw, prev, p):
    rows = rw.shape[0]
    rowspec = lambda w: pl.BlockSpec((rows, w), lambda i: (0, 0))
    const = lambda a: pl.BlockSpec(a.shape, lambda i: (0,) * a.ndim)
    params = [p[n] for n in _MIX_PARAMS]
    return pl.pallas_call(
        _rwkv_mix_kernel,
        grid=(1,),
        in_specs=[rowspec(SHIFT_W), rowspec(SHIFT_W)] + [const(a) for a in params],
        out_specs=[rowspec(RW_W)] * 7,
        out_shape=[jax.ShapeDtypeStruct((rows, RW_W), F32)] * 7,
        compiler_params=pltpu.CompilerParams(dimension_semantics=("arbitrary",), vmem_limit_bytes=VMEM_LIMIT),
        name="rwkv7_mix",
    )(rw, prev, *params)


def _rwkv_steps_kernel(r_ref, w_ref, k_ref, v_ref, a_ref, b_ref, g_ref, s_ref, lnw_ref, lnb_ref, rk_ref,
                       y_ref, so_ref, o_scr, *, n_steps):
    def value_row(vi, carry):
        s = s_ref[0, vi]
        for t in range(n_steps):
            sa = jnp.sum(s * a_ref[t, 0], axis=0, keepdims=True)
            s = s * w_ref[t, 0] + sa * b_ref[t, 0] + v_ref[t, 0, pl.ds(vi, 1), :] * k_ref[t, 0]
            o_scr[t, pl.ds(vi, 1), :] = jnp.sum(s * r_ref[t, 0], axis=0, keepdims=True)
        so_ref[0, vi] = s
        return carry

    lax.fori_loop(0, RW_N, value_row, 0)
    for t in range(n_steps):
        o = o_scr[t]
        d = o - jnp.mean(o, axis=0, keepdims=True)
        var = jnp.mean(d * d, axis=0, keepdims=True)
        bonus = jnp.sum(r_ref[t, 0] * k_ref[t, 0] * rk_ref[0], axis=0, keepdims=True)
        o = d * lax.rsqrt(var + GN_EPS) * lnw_ref[0] + lnb_ref[0]
        y_ref[t, 0] = (o + bonus * v_ref[t, 0]) * g_ref[t, 0]


def _rwkv_steps(vecs, state, lnw, lnb, rk):
    n_steps, _, _, n_batch = vecs[0].shape
    vec = pl.BlockSpec((n_steps, 1, RW_N, n_batch), lambda h: (0, h, 0, 0))
    st = pl.BlockSpec((1, RW_N, RW_N, n_batch), lambda h: (h, 0, 0, 0))
    par = pl.BlockSpec((1, RW_N, n_batch), lambda h: (h, 0, 0))
    return pl.pallas_call(
        functools.partial(_rwkv_steps_kernel, n_steps=n_steps),
        grid=(RW_HEADS,),
        in_specs=[vec] * 7 + [st, par, par, par],
        out_specs=[vec, st],
        out_shape=[jax.ShapeDtypeStruct(vecs[0].shape, F32), jax.ShapeDtypeStruct(state.shape, F32)],
        scratch_shapes=[pltpu.VMEM((n_steps, RW_N, n_batch), F32)],
        compiler_params=pltpu.CompilerParams(dimension_semantics=("arbitrary",), vmem_limit_bytes=VMEM_LIMIT),
        name="rwkv7_steps",
    )(*vecs, state, lnw, lnb, rk)


def _rms(x, g):
    return x * lax.rsqrt(jnp.mean(x * x, axis=-1, keepdims=True) + NORM_EPS) * g


def _merge_ffn_kernel(x_ref, ya_ref, yr_ref, gate_ref, wa_ref, wr_ref, wo_ref, wu_ref, wd_ref,
                      n1_ref, n2pre_ref, n2_ref, out_ref):
    d = functools.partial(jnp.dot, preferred_element_type=F32)
    merged = gate_ref[:, :D_MODEL] * d(ya_ref[...], wa_ref[...]) + gate_ref[:, D_MODEL:] * d(yr_ref[...], wr_ref[...])
    x1 = x_ref[...] + _rms(d(merged.astype(BF16), wo_ref[...]), n1_ref[...])
    h2 = _rms(x1, n2pre_ref[...]).astype(BF16)
    u = jnp.square(jnp.maximum(d(h2, wu_ref[...]), 0.0))
    out_ref[...] = x1 + _rms(d(u.astype(BF16), wd_ref[...]), n2_ref[...])


def _merge_ffn(x, y_att, y_rw, gates, wa, wr, wo, wu, wd, n1_post, n2_pre, n2_post):
    rows = x.shape[0]
    tm = 256
    row = lambda w: pl.BlockSpec((tm, w), lambda i: (i, 0))
    const = lambda a: pl.BlockSpec(a.shape, lambda i: (0, 0), pipeline_mode=pl.Buffered(1))
    consts = [wa, wr, wo, wu, wd, n1_post, n2_pre, n2_post]
    return pl.pallas_call(
        _merge_ffn_kernel,
        grid=(rows // tm,),
        in_specs=[row(D_MODEL), row(ATT_W), row(RW_W), row(GATE_W)] + [const(a) for a in consts],
        out_specs=row(D_MODEL),
        out_shape=jax.ShapeDtypeStruct((rows, D_MODEL), F32),
        compiler_params=pltpu.CompilerParams(dimension_semantics=("arbitrary",), vmem_limit_bytes=VMEM_LIMIT),
        name="merge_ffn",
    )(x, y_att, y_rw, gates, *consts)


PAGES_PER_BLOCK = MOBA_BLOCK // PAGE_SIZE
WAVE = 16
SELECT_SLOTS = 4
ATTEND_SLOTS = 4
SEL_BITS = 5


def _select_step(step, pt_ref, qt_ref, ck_ref, buf, sem, sc_ref, *, n_pages, n_batch, dec_seq, waves_per_step):
    waves_per_b = n_pages // WAVE
    n_waves = n_batch * waves_per_b
    assert SELECT_SLOTS >= 2 * waves_per_step and waves_per_b % waves_per_step == 0

    def copies(g):
        bb, w, slot = g // waves_per_b, g % waves_per_b, g % SELECT_SLOTS
        return [pltpu.make_async_copy(ck_ref.at[pt_ref[bb, w * WAVE + i]], buf.at[slot, i], sem.at[slot])
                for i in range(WAVE)]

    @pl.when(step == 0)
    def _():
        for g in range(waves_per_step):
            for cp in copies(g):
                cp.start()

    first = step * waves_per_step
    for k in range(waves_per_step):
        @pl.when(first + waves_per_step + k < n_waves)
        def _():
            for cp in copies(first + waves_per_step + k):
                cp.start()

    for k in range(waves_per_step):
        for cp in copies(first + k):
            cp.wait()

    qt = qt_ref[0]
    lane = lax.broadcasted_iota(jnp.int32, (HEAD_DIM, LANES), 1)
    for k in range(waves_per_step):
        g = first + k
        slot = g % SELECT_SLOTS
        for i in range(0, WAVE, PAGES_PER_BLOCK):
            msel = jnp.zeros((HEAD_DIM, LANES), F32)
            for hh in range(N_HEADS):
                t = buf[slot, i, hh]
                for j in range(1, PAGES_PER_BLOCK):
                    t = t + buf[slot, i + j, hh]
                ksum = jnp.sum(t, axis=1, keepdims=True)
                msel = jnp.where(lane // dec_seq == hh, ksum, msel)
            n = ((g % waves_per_b) * WAVE + i) // PAGES_PER_BLOCK
            sc_ref[pl.ds(n, 1), :] = jnp.sum(msel * qt, axis=0, keepdims=True) * (1.0 / MOBA_BLOCK)


def _select_finish(step, sel_ref, sc_ref, *, n_pages, n_batch, dec_seq, waves_per_step):
    steps_per_b = (n_pages // WAVE) // waves_per_step
    n_full = n_pages // PAGES_PER_BLOCK

    @pl.when(step % steps_per_b == steps_per_b - 1)
    def _():
        sc = sc_ref[...]
        blk = lax.broadcasted_iota(jnp.int32, (n_full, LANES), 0)
        row = lax.broadcasted_iota(jnp.int32, (8, LANES), 0)
        out = jnp.zeros((8, LANES), jnp.int32)
        for r in range(MOBA_TOPK):
            m = jnp.max(sc, axis=0, keepdims=True)
            idx = jnp.min(jnp.where(sc == m, blk, n_full), axis=0, keepdims=True)
            out = jnp.where(row == r, idx, out)
            sc = jnp.where(blk == idx, -jnp.inf, sc)
        sel_ref[0] = out


def _sample_attend_kernel(pt_ref, sel_ref, wq_ref, kn_ref, vn_ref, ck_ref, cv_ref, o_ref, kbuf, vbuf, sem,
                          *, n_batch, dec_seq):
    b = pl.program_id(0)
    h = pl.program_id(1)
    seg = MOBA_TOPK * MOBA_BLOCK

    def copies(g):
        bb, hh, slot = g // N_HEADS, g % N_HEADS, g % ATTEND_SLOTS
        cps = []
        for qi in range(dec_seq):
            packed = sel_ref[bb, hh * dec_seq + qi]
            for r in range(MOBA_TOPK):
                blk = (packed >> (SEL_BITS * r)) & ((1 << SEL_BITS) - 1)
                for pg in range(PAGES_PER_BLOCK):
                    page = pt_ref[bb, blk * PAGES_PER_BLOCK + pg]
                    t = (qi * MOBA_TOPK + r) * PAGES_PER_BLOCK + pg
                    dst = pl.ds(t * PAGE_SIZE, PAGE_SIZE)
                    cps.append(pltpu.make_async_copy(ck_ref.at[page, hh], kbuf.at[slot, :, dst], sem.at[slot, 0]))
                    cps.append(pltpu.make_async_copy(cv_ref.at[page, hh], vbuf.at[slot, :, dst], sem.at[slot, 1]))
        return cps

    g = b * N_HEADS + h
    slot = g % ATTEND_SLOTS

    @pl.when(g == 0)
    def _():
        for g0 in range(ATTEND_SLOTS - 1):
            for cp in copies(g0):
                cp.start()

    @pl.when(g + ATTEND_SLOTS - 1 < n_batch * N_HEADS)
    def _():
        for cp in copies(g + ATTEND_SLOTS - 1):
            cp.start()

    for cp in copies(g):
        cp.wait()

    wq = wq_ref[0]
    kn = kn_ref[0]
    vn = vn_ref[0]
    lane_d = lax.broadcasted_iota(jnp.int32, (HEAD_DIM, LANES), 1)
    lane_1 = lax.broadcasted_iota(jnp.int32, (1, LANES), 1)
    @pl.when(h == 0)
    def _():
        o_ref[...] = jnp.zeros_like(o_ref)

    out = o_ref[0]
    for qi in range(dec_seq):
        me = h * dec_seq + qi
        qcol = jnp.sum(jnp.where(lane_d == me, wq, 0.0), axis=1, keepdims=True)
        keys = slice(qi * seg, (qi + 1) * seg)
        s = jnp.sum(kbuf[slot, :, keys] * qcol, axis=0, keepdims=True)
        s_own = jnp.sum(kn * qcol, axis=0, keepdims=True)
        own = jnp.logical_and(lane_1 // dec_seq == h, lane_1 % dec_seq <= qi)
        s_own = jnp.where(own, s_own, NEG_INF)
        m = jnp.maximum(jnp.max(s, axis=1, keepdims=True), jnp.max(s_own, axis=1, keepdims=True))
        p = jnp.exp(s - m)
        p_own = jnp.exp(s_own - m)
        l = jnp.sum(p, axis=1, keepdims=True) + jnp.sum(p_own, axis=1, keepdims=True)
        o = jnp.sum(vbuf[slot, :, keys] * p, axis=1, keepdims=True) + jnp.sum(vn * p_own, axis=1, keepdims=True)
        out = jnp.where(lane_d == me, o * (1.0 / l), out)
    o_ref[0] = out


def _sample_attend(page_table, sel_packed, wq, kn_t, vn_t, ck_view, cv_view, dec_seq):
    n_batch = page_table.shape[0]
    n_keys = dec_seq * MOBA_TOPK * MOBA_BLOCK
    nblk = pl.BlockSpec((1, HEAD_DIM, LANES), lambda b, h, pt, sel: (b, 0, 0))
    return pl.pallas_call(
        functools.partial(_sample_attend_kernel, n_batch=n_batch, dec_seq=dec_seq),
        grid_spec=pltpu.PrefetchScalarGridSpec(
            num_scalar_prefetch=2,
            grid=(n_batch, N_HEADS),
            in_specs=[nblk, nblk, nblk, pl.BlockSpec(memory_space=pl.ANY), pl.BlockSpec(memory_space=pl.ANY)],
            out_specs=nblk,
            scratch_shapes=[pltpu.VMEM((ATTEND_SLOTS, HEAD_DIM, n_keys), F32),
                            pltpu.VMEM((ATTEND_SLOTS, HEAD_DIM, n_keys), F32),
                            pltpu.SemaphoreType.DMA((ATTEND_SLOTS, 2))],
        ),
        out_shape=jax.ShapeDtypeStruct((n_batch, HEAD_DIM, LANES), F32),
        compiler_params=pltpu.CompilerParams(dimension_semantics=("arbitrary", "arbitrary"),
                                             vmem_limit_bytes=VMEM_LIMIT),
        name="sample_attend",
    )(page_table, sel_packed, wq, kn_t, vn_t, ck_view, cv_view)


def kernel(x_prompt, x_sample, cache_k, cache_v, state_wkv, state_shift, page_table, n1_pre, n1_post, n2_pre, n2_post, w_in, rw_mu, rw_w0, rw_w2, rw_a0, rw_a2, rw_g2, rw_kk, rw_ka, rw_rk, rw_ln_w, rw_ln_b, w_att_br, w_rw_br, w_out, w_up, w_down):
    _, seq, _ = x_prompt.shape
    n_batch, dec_seq, _ = x_sample.shape
    n_pages = page_table.shape[1]
    past = n_pages * PAGE_SIZE
    assert past % MOBA_BLOCK == 0 and n_pages // PAGES_PER_BLOCK >= MOBA_TOPK
    assert n_pages // PAGES_PER_BLOCK <= (1 << SEL_BITS) and x_prompt.shape[0] == 1 and n1_pre.shape[0] == 1
    l = 0
    row2 = lambda a: a[l].reshape(1, -1)
    rwp = dict(rw_mu=row2(rw_mu), rw_w0=row2(rw_w0), rw_w2=rw_w2[l], rw_a0=row2(rw_a0), rw_a2=rw_a2[l],
               rw_g2=rw_g2[l], rw_kk=row2(rw_kk), rw_ka=row2(rw_ka), rw_rk=row2(rw_rk),
               rw_ln_w=row2(rw_ln_w), rw_ln_b=row2(rw_ln_b))
    w_in_b = w_in[l].astype(BF16)
    ffn_w = [w.astype(BF16) for w in (w_att_br[l], w_rw_br[l], w_out[l], w_up[l], w_down[l])]
    norms = [row2(n1_post), row2(n2_pre), row2(n2_post)]
    heads_t = lambda t: t.reshape(N_HEADS, HEAD_DIM, -1)

    xs = x_sample.reshape(n_batch * dec_seq, D_MODEL)
    cos_s, sin_s = _rope_tables(jnp.tile(past + jnp.arange(dec_seq), n_batch))
    qs, _, _, kt_s, vt_s, rw_s, gate_s, _ = _project(xs, cos_s, sin_s, row2(n1_pre), w_in_b)
    kn_t = heads_t(kt_s).reshape(N_HEADS, HEAD_DIM, n_batch, dec_seq)
    vn_t = heads_t(vt_s).reshape(N_HEADS, HEAD_DIM, n_batch, dec_seq)
    q4 = qs.reshape(n_batch, dec_seq, N_HEADS, HEAD_DIM)
    q_t = q4.transpose(0, 3, 2, 1).reshape(n_batch, HEAD_DIM, N_HEADS * dec_seq)
    q_t = jnp.pad(q_t, ((0, 0), (0, 0), (0, LANES - N_HEADS * dec_seq)))
    ck_view = jnp.swapaxes(cache_k[:, l], -1, -2)
    cv_view = jnp.swapaxes(cache_v[:, l], -1, -2)

    xp = x_prompt[0]
    cos, sin = _rope_tables(jnp.arange(seq))
    q, kb, vtb, kt, vt, rw, gate, kmean = _project(xp, cos, sin, row2(n1_pre), w_in_b)
    n_blk = seq // MOBA_BLOCK
    y_att = _moba_prompt(q, kb.reshape(n_blk, MOBA_BLOCK, ATT_W), vtb, kmean.reshape(n_blk, ATT_W))
    y_rw, wkv_p, sel = _rwkv_prompt_and_select(rw, rwp, page_table, q_t, ck_view, dec_seq)
    y_prompt = _merge_ffn(xp, y_att, y_rw, gate, *ffn_w, *norms)
    k_prompt = heads_t(kt).transpose(0, 2, 1)[None, None]
    v_prompt = heads_t(vt).transpose(0, 2, 1)[None, None]

    sel = sel[:, :MOBA_TOPK, :N_HEADS * dec_seq]
    sel_packed = sel[:, 0] | (sel[:, 1] << SEL_BITS) | (sel[:, 2] << (2 * SEL_BITS))
    pad_new = lambda t: jnp.pad(t.transpose(2, 1, 0, 3).reshape(n_batch, HEAD_DIM, N_HEADS * dec_seq),
                                ((0, 0), (0, 0), (0, LANES - N_HEADS * dec_seq)))
    o_s = _sample_attend(page_table, sel_packed, q_t * (HEAD_DIM ** -0.5), pad_new(kn_t), pad_new(vn_t),
                         ck_view, cv_view, dec_seq)
    o_s = o_s[:, :, :N_HEADS * dec_seq].reshape(n_batch, HEAD_DIM, N_HEADS, dec_seq)
    y_att_s = o_s.transpose(0, 3, 2, 1).reshape(n_batch * dec_seq, ATT_W).astype(BF16)

    rw_s3 = rw_s.reshape(n_batch, dec_seq, SHIFT_W)
    prev_s = jnp.concatenate([state_shift[:, l][:, None], rw_s3[:, :-1]], axis=1).reshape(n_batch * dec_seq, SHIFT_W)
    to_steps = lambda t: t.reshape(n_batch, dec_seq, RW_HEADS, RW_N).transpose(1, 2, 3, 0)
    vecs = [to_steps(t) for t in _rwkv_mix_call(rw_s, prev_s, rwp)]
    on_lanes = lambda a: jnp.broadcast_to(a[l].reshape(RW_HEADS, RW_N, 1), (RW_HEADS, RW_N, n_batch))
    y_steps, wkv_s = _rwkv_steps(vecs, state_wkv[:, l].transpose(1, 2, 3, 0),
                                 on_lanes(rw_ln_w), on_lanes(rw_ln_b), on_lanes(rw_rk))
    y_rw_s = y_steps.transpose(3, 0, 1, 2).reshape(n_batch * dec_seq, RW_W).astype(BF16)
    y_sample = _merge_ffn(xs, y_att_s, y_rw_s, gate_s, *ffn_w, *norms)
    k_sample = kn_t.transpose(2, 0, 3, 1)[:, None]
    v_sample = vn_t.transpose(2, 0, 3, 1)[:, None]

    return (y_prompt[None], y_sample.reshape(n_batch, dec_seq, D_MODEL), k_prompt, v_prompt, k_sample, v_sample,
            wkv_p[None, None], wkv_s.transpose(3, 0, 1, 2)[:, None], rw[-1][None, None], rw_s3[:, -1][:, None])
```

```python
import functools
import math

import jax
import jax.numpy as jnp
from jax import lax
from jax.experimental import pallas as pl
from jax.experimental.pallas import tpu as pltpu

F32 = jnp.float32
BF16 = jnp.bfloat16

D_MODEL = 1024
N_HEADS = 8
HEAD_DIM = 64
ATT_W = N_HEADS * HEAD_DIM
RW_HEADS = 8
RW_N = 64
RW_W = RW_HEADS * RW_N
DECAY_LORA = 64
AAA_LORA = 64
GATE_LORA = 128
D_FF = 4 * D_MODEL
MOBA_BLOCK = 256
MOBA_TOPK = 3
Q_BLOCK = 128
PAGE_SIZE = 128
ROPE_THETA = 10000.0
NORM_EPS = 1e-6
GN_EPS = 64e-5
NEG_INF = -1e30
LOG2_E = 1.4426950408889634
SHIFT_W = 3 * RW_W + DECAY_LORA + AAA_LORA + GATE_LORA
GATE_W = 2 * D_MODEL
IN_W = 3 * ATT_W + SHIFT_W + GATE_W

LANES = 128
SUBLANES = 8
ONES_ROWS = 16
MOBA_GROUP = 4
MOBA_COLS = (MOBA_BLOCK // Q_BLOCK) * 2 * Q_BLOCK
VMEM_LIMIT = 56 * 1024 * 1024

NT_DIMS = (((1,), (1,)), ((), ()))
TN_DIMS = (((0,), (0,)), ((), ()))


def _split2(x):
    hi = x.astype(BF16)
    lo = (x - hi.astype(F32)).astype(BF16)
    return hi, lo


def _dot3(a, b, dims=(((1,), (0,)), ((), ()))):
    ah, al = _split2(a)
    bh, bl = _split2(b)
    d = functools.partial(lax.dot_general, dimension_numbers=dims, preferred_element_type=F32)
    return d(ah, bh) + (d(ah, bl) + d(al, bh))


def _split3(x):
    x1 = x.astype(BF16)
    r1 = x - x1.astype(F32)
    x2 = r1.astype(BF16)
    x3 = (r1 - x2.astype(F32)).astype(BF16)
    return x1, x2, x3


def _dot_exact_lhs(mask, b):
    a = mask.astype(BF16)
    b1, b2, b3 = _split3(b)
    d = functools.partial(jnp.dot, preferred_element_type=F32)
    return d(a, b1) + (d(a, b2) + d(a, b3))


def _head_sums(x):
    li = lax.broadcasted_iota(jnp.int32, (LANES, LANES), 0) // RW_N
    lj = lax.broadcasted_iota(jnp.int32, (LANES, LANES), 1) // RW_N
    ones = (li == lj).astype(BF16)
    d = functools.partial(jnp.dot, preferred_element_type=F32)
    out = []
    for c in range(RW_W // LANES):
        hi, lo = _split2(x[:, c * LANES:(c + 1) * LANES])
        out.append(d(hi, ones) + d(lo, ones))
    return jnp.concatenate(out, axis=1)


def _sigmoid(x):
    return 1.0 / (1.0 + jnp.exp(-x))


def _select_past_blocks(q_tile, kmean_tile, ob):
    rows, n_blk = q_tile.shape[0], kmean_tile.shape[0]
    lane = lax.broadcasted_iota(jnp.int32, (rows, LANES), 1)
    blk = lax.broadcasted_iota(jnp.int32, (n_blk, rows), 0)
    out = []
    for hh in range(2):
        sb = _dot3(kmean_tile, jnp.where((lane // HEAD_DIM) == hh, q_tile, 0.0), NT_DIMS)
        sb = jnp.where(blk < ob, sb, NEG_INF)
        sel = jnp.zeros((n_blk, rows), jnp.bool_)
        for _ in range(MOBA_TOPK):
            m = jnp.max(sb, axis=0, keepdims=True)
            idx = jnp.min(jnp.where(sb == m, blk, n_blk), axis=0, keepdims=True)
            pick = blk == idx
            sel = jnp.logical_or(sel, pick)
            sb = jnp.where(pick, -jnp.inf, sb)
        out.append(jnp.logical_and(sel, blk < ob).astype(F32))
    return jnp.concatenate(out, axis=1)


def _proj_kernel(x_ref, cos_ref, sin_ref, g_ref, w_ref,
                 q_ref, kb_ref, vtb_ref, kt_ref, vt_ref, rw_ref, gate_ref, *select_refs, tm):
    i = pl.program_id(0)
    if select_refs:
        sel_ref, km_ref = select_refs

        @pl.when(i == 0)
        def _():
            km_ref[...] = jnp.zeros_like(km_ref)

    x = x_ref[...]
    ms = jnp.mean(x * x, axis=-1, keepdims=True)
    h = (x * lax.rsqrt(ms + NORM_EPS) * g_ref[...]).astype(BF16)
    cos = cos_ref[...]
    sin = sin_ref[...]
    lane = lax.broadcasted_iota(jnp.int32, (tm, LANES), 1)
    first_half = (lane % HEAD_DIM) < (HEAD_DIM // 2)

    def rope(p):
        sw = jnp.where(first_half, pltpu.roll(p, LANES - HEAD_DIM // 2, 1), pltpu.roll(p, HEAD_DIM // 2, 1))
        return p * cos + sw * sin

    def proj(lo, width):
        return jnp.dot(h, w_ref[:, lo:lo + width], preferred_element_type=F32)

    pq = proj(0, ATT_W)
    for c in range(ATT_W // LANES):
        cols = slice(c * LANES, (c + 1) * LANES)
        qr = rope(pq[:, cols])
        q_ref[:, cols] = qr
        if select_refs:
            sel_ref[0, :, 2 * tm * c:2 * tm * (c + 1)] = _select_past_blocks(qr, km_ref[:, cols], i)
    pk = proj(ATT_W, ATT_W)
    k_means = []
    for c in range(ATT_W // LANES):
        cols = slice(c * LANES, (c + 1) * LANES)
        kr = rope(pk[:, cols])
        kb_ref[:, cols] = kr.astype(BF16)
        kt_ref[cols, :] = kr.T
        k_means.append(jnp.sum(kr, axis=0, keepdims=True) * (1.0 / tm))
    if select_refs:
        km_ref[pl.ds(i, 1), :] = jnp.concatenate(k_means, axis=1)
    pv = proj(2 * ATT_W, ATT_W)
    for c in range(ATT_W // LANES):
        cols = slice(c * LANES, (c + 1) * LANES)
        vt = pv[:, cols].T
        vt_ref[cols, :] = vt
        vtb_ref[0, cols, :] = vt.astype(BF16)
    rw_ref[...] = proj(3 * ATT_W, SHIFT_W)
    gate_ref[...] = _sigmoid(proj(3 * ATT_W + SHIFT_W, GATE_W))


def _rope_tables(pos):
    half = HEAD_DIM // 2
    inv = ROPE_THETA ** (-jnp.arange(half, dtype=F32) / half)
    ang = pos.astype(F32)[:, None] * inv[None, :]
    cos = jnp.cos(ang)
    sin = jnp.sin(ang)
    return jnp.tile(jnp.concatenate([cos, cos], axis=1), (1, 2)), jnp.tile(jnp.concatenate([-sin, sin], axis=1), (1, 2))


def _project(x, cos, sin, g, w_in_bf16, moba_select):
    rows = x.shape[0]
    tm = MOBA_BLOCK
    n = rows // tm
    n_sel = N_HEADS * tm
    row = lambda w: pl.BlockSpec((tm, w), lambda i: (i, 0))
    const = lambda a, b: pl.BlockSpec((a, b), lambda i: (0, 0))
    col = pl.BlockSpec((ATT_W, tm), lambda i: (0, i))
    out_specs = [row(ATT_W), row(ATT_W), pl.BlockSpec((1, ATT_W, tm), lambda i: (i, 0, 0)), col, col,
                 row(SHIFT_W), row(GATE_W)]
    out_shape = [
        jax.ShapeDtypeStruct((rows, ATT_W), F32),
        jax.ShapeDtypeStruct((rows, ATT_W), BF16),
        jax.ShapeDtypeStruct((n, ATT_W, tm), BF16),
        jax.ShapeDtypeStruct((ATT_W, rows), F32),
        jax.ShapeDtypeStruct((ATT_W, rows), F32),
        jax.ShapeDtypeStruct((rows, SHIFT_W), F32),
        jax.ShapeDtypeStruct((rows, GATE_W), F32),
    ]
    scratch = []
    if moba_select:
        out_specs.append(pl.BlockSpec((1, n, n_sel), lambda i: (i, 0, 0)))
        out_shape.append(jax.ShapeDtypeStruct((n, n, n_sel), F32))
        scratch.append(pltpu.VMEM((n, ATT_W), F32))
    return pl.pallas_call(
        functools.partial(_proj_kernel, tm=tm),
        grid=(n,),
        in_specs=[row(D_MODEL), row(LANES), row(LANES), const(1, D_MODEL), const(D_MODEL, IN_W)],
        out_specs=out_specs,
        out_shape=out_shape,
        scratch_shapes=scratch,
        compiler_params=pltpu.CompilerParams(dimension_semantics=("arbitrary",), vmem_limit_bytes=VMEM_LIMIT),
        name="project",
    )(x, cos, sin, g, w_in_bf16)


def _moba_prompt_kernel(q_ref, k_ref, vt_ref, sel_in_ref, y_ref, sa_ref, sb_ref, *, n_blk):
    ob = pl.program_id(1)
    n_ch = MOBA_BLOCK // Q_BLOCK
    sel_ref = sel_in_ref.at[0]
    lane = lax.broadcasted_iota(jnp.int32, (Q_BLOCK, LANES), 1)
    qs = [None] * (2 * n_ch)
    for ch in range(n_ch):
        q = q_ref[ch * Q_BLOCK:(ch + 1) * Q_BLOCK, :]
        for hh in range(2):
            qh = jnp.where((lane // HEAD_DIM) == hh, q, 0.0)
            qs[hh * n_ch + ch] = (qh * (HEAD_DIM ** -0.5 * LOG2_E)).astype(BF16)
    qs = jnp.concatenate(qs, axis=0)

    def scores(j):
        return lax.dot_general(k_ref[j], qs, NT_DIMS, preferred_element_type=F32)

    s = scores(ob)
    key = lax.broadcasted_iota(jnp.int32, (MOBA_BLOCK, MOBA_COLS), 0)
    col = lax.broadcasted_iota(jnp.int32, (MOBA_BLOCK, MOBA_COLS), 1)
    s = jnp.where(key <= (col // Q_BLOCK % n_ch) * Q_BLOCK + col % Q_BLOCK, s, NEG_INF)
    m0 = jnp.max(s, axis=0, keepdims=True)
    ones_rows = jnp.ones((ONES_ROWS, MOBA_BLOCK), BF16)

    def pv(j, p):
        vt = vt_ref[j]
        pb = p.astype(BF16)
        per_head = MOBA_COLS // 2
        return jnp.concatenate(
            [jnp.dot(jnp.concatenate([vt[hh * HEAD_DIM:(hh + 1) * HEAD_DIM], ones_rows], axis=0),
                     pb[:, hh * per_head:(hh + 1) * per_head], preferred_element_type=F32) for hh in range(2)],
            axis=1)

    acc0 = pv(ob, jnp.exp2(s - m0))

    def block(g, t):
        return jnp.minimum(MOBA_GROUP * g + t, n_blk - 1)

    def masked_scores(j):
        return jnp.where(sel_ref[pl.ds(j, 1), :] > 0.0, scores(j), NEG_INF)

    def group_step(g, cur_ref, nxt_ref, m_cur, m_acc, acc):
        mx = jnp.full((1, MOBA_COLS), NEG_INF, F32)
        tot = None
        for t in range(MOBA_GROUP):
            sn = masked_scores(block(g + 1, t))
            nxt_ref[t] = sn
            mx = jnp.maximum(mx, jnp.max(sn, axis=0, keepdims=True))
            part = pv(block(g, t), jnp.exp2(cur_ref[t] - m_cur))
            tot = part if tot is None else tot + part
        return jnp.maximum(m_cur, mx), m_cur, acc * jnp.exp2(m_acc - m_cur) + tot

    mx = m0
    for t in range(MOBA_GROUP):
        s = masked_scores(block(0, t))
        sa_ref[t] = s
        mx = jnp.maximum(mx, jnp.max(s, axis=0, keepdims=True))

    def body(i, carry):
        m_cur, m_acc, acc = group_step(2 * i, sa_ref, sb_ref, *carry)
        return group_step(2 * i + 1, sb_ref, sa_ref, m_cur, m_acc, acc)

    n_groups = (ob + MOBA_GROUP - 1) // MOBA_GROUP
    _, _, acc = lax.fori_loop(0, (n_groups + 1) // 2, body, (mx, m0, acc0))
    o = acc[:HEAD_DIM] * (1.0 / acc[HEAD_DIM:HEAD_DIM + 1])
    for ch in range(n_ch):
        ot = jnp.concatenate([o[:, (hh * n_ch + ch) * Q_BLOCK:(hh * n_ch + ch + 1) * Q_BLOCK] for hh in range(2)],
                             axis=0)
        y_ref[ch * Q_BLOCK:(ch + 1) * Q_BLOCK, :] = ot.T.astype(y_ref.dtype)


def _moba_prompt(q, k_blocks, vt_blocks, sel):
    s = q.shape[0]
    n_blk = k_blocks.shape[0]
    return pl.pallas_call(
        functools.partial(_moba_prompt_kernel, n_blk=n_blk),
        grid=(ATT_W // LANES, n_blk),
        in_specs=[
            pl.BlockSpec((MOBA_BLOCK, LANES), lambda hp, c: (c, hp)),
            pl.BlockSpec((n_blk, MOBA_BLOCK, LANES), lambda hp, c: (0, 0, hp)),
            pl.BlockSpec((n_blk, LANES, MOBA_BLOCK), lambda hp, c: (0, hp, 0)),
            pl.BlockSpec((1, n_blk, MOBA_COLS), lambda hp, c: (c, 0, hp)),
        ],
        out_specs=pl.BlockSpec((MOBA_BLOCK, LANES), lambda hp, c: (c, hp)),
        out_shape=jax.ShapeDtypeStruct((s, ATT_W), BF16),
        scratch_shapes=[pltpu.VMEM((MOBA_GROUP, MOBA_BLOCK, MOBA_COLS), F32),
                        pltpu.VMEM((MOBA_GROUP, MOBA_BLOCK, MOBA_COLS), F32)],
        compiler_params=pltpu.CompilerParams(dimension_semantics=("arbitrary", "arbitrary"),
                                             vmem_limit_bytes=VMEM_LIMIT),
        name="moba_prompt",
    )(q, k_blocks, vt_blocks, sel)


def _rwkv_mix(rw, prev, mu_ref, w0_ref, w2_ref, a0_ref, a2_ref, g2_ref, kk_ref, ka_ref):
    m = rw + (prev - rw) * mu_ref[...]
    r = m[:, 0:RW_W]
    k = m[:, RW_W:2 * RW_W]
    v = m[:, 2 * RW_W:3 * RW_W]
    o1 = 3 * RW_W
    wd = m[:, o1:o1 + DECAY_LORA]
    ad = m[:, o1 + DECAY_LORA:o1 + DECAY_LORA + AAA_LORA]
    gd = m[:, o1 + DECAY_LORA + AAA_LORA:]
    bdot = lambda a, b: jnp.dot(a.astype(BF16), b.astype(BF16), preferred_element_type=F32)
    w_raw = w0_ref[...] + bdot(jnp.tanh(wd), w2_ref[...])
    logw = -math.exp(-0.5) * _sigmoid(w_raw)
    a_sig = _sigmoid(a0_ref[...] + bdot(ad, a2_ref[...]))
    g = bdot(_sigmoid(gd), g2_ref[...])

    kk = k * kk_ref[...]
    kk_norm = jnp.sqrt(_head_sums(kk * kk))
    kk = kk / jnp.maximum(kk_norm, 1e-12)
    k2 = k * (1.0 + (a_sig - 1.0) * ka_ref[...])
    return r, logw, k2, v, -kk, kk * a_sig, g


def _rwkv_kernel(pt_ref, rw_ref, tail_ref, mu_ref, w0_ref, w2_ref, a0_ref, a2_ref, g2_ref, kk_ref, ka_ref,
                 rk_ref, lnw_ref, lnb_ref, qt_ref, ck_ref, y_ref, s_ref, sel_ref, buf, sem, sc_ref,
                 *, chunk, select):
    @pl.when(pl.program_id(0) == 0)
    def _():
        s_ref[...] = jnp.zeros_like(s_ref)

    _select_step(pl.program_id(0), pt_ref, qt_ref, ck_ref, buf, sem, sc_ref, **select)

    rw = rw_ref[...]
    before = jnp.where(pl.program_id(0) == 0, 0.0, tail_ref[SUBLANES - 1:SUBLANES, :])
    first_row = lax.broadcasted_iota(jnp.int32, rw.shape, 0) == 0
    prev = jnp.where(first_row, before, pltpu.roll(rw, 1, 0))
    r, logw, k2, v, a_vec, b_vec, g = _rwkv_mix(rw, prev, mu_ref, w0_ref, w2_ref, a0_ref, a2_ref,
                                                g2_ref, kk_ref, ka_ref)
    ti = lax.broadcasted_iota(jnp.int32, (chunk, chunk), 0)
    tj = lax.broadcasted_iota(jnp.int32, (chunk, chunk), 1)
    lower_incl = tj <= ti
    lower_strict = tj < ti
    cs = _dot_exact_lhs(lower_incl.astype(F32), logw)
    p_incl = jnp.exp(cs)
    p_inv = jnp.exp(-cs)
    a_t = a_vec * jnp.exp(cs - logw)
    r_t = r * p_incl
    b_t = b_vec * p_inv
    k_t = k2 * p_inv
    p_last = p_incl[chunk - 1:chunk, :]
    eye = (ti == tj).astype(F32)

    heads = range(RW_HEADS)
    sl = [slice(h * RW_N, (h + 1) * RW_N) for h in heads]
    ar = [jnp.concatenate([a_t[:, s], r_t[:, s]], axis=0) for s in sl]
    bk = [jnp.concatenate([b_t[:, s], k_t[:, s]], axis=0) for s in sl]
    gram = [_dot3(ar[h], bk[h], NT_DIMS) for h in heads]
    s0 = [s_ref[h] for h in heads]
    ars = [_dot3(ar[h], s0[h], NT_DIMS) for h in heads]
    l_ak_v = [_dot3(jnp.where(lower_strict, gram[h][:chunk, chunk:], 0.0), v[:, sl[h]]) for h in heads]
    l_ab = [jnp.where(lower_strict, gram[h][:chunk, :chunk], 0.0) for h in heads]
    x = [eye + l_ab[h] for h in heads]
    pw = [_dot3(l_ab[h], l_ab[h]) for h in heads]
    n_factors = int(math.log2(chunk))
    for it in range(1, n_factors):
        if it + 1 < n_factors:
            prod = [_dot3(pw[h], jnp.concatenate([pw[h], x[h]], axis=1)) for h in heads]
            pw = [prod[h][:, :chunk] for h in heads]
            x = [x[h] + prod[h][:, chunk:] for h in heads]
        else:
            x = [x[h] + _dot3(pw[h], x[h]) for h in heads]
    u = [_dot3(x[h], ars[h][:chunk] + l_ak_v[h]) for h in heads]
    uv = [jnp.concatenate([u[h], v[:, sl[h]]], axis=0) for h in heads]
    wide_i = lax.broadcasted_iota(jnp.int32, (chunk, 2 * chunk), 0)
    wide_j = lax.broadcasted_iota(jnp.int32, (chunk, 2 * chunk), 1) % chunk
    m_r = [jnp.where(wide_j <= wide_i, gram[h][chunk:, :], 0.0) for h in heads]
    ys = [ars[h][chunk:] + _dot3(m_r[h], uv[h]) for h in heads]
    for h in heads:
        s_ref[h] = (s0[h] + _dot3(uv[h], bk[h], TN_DIMS)) * p_last[:, sl[h]]
    o = jnp.concatenate(ys, axis=1)

    mean = _head_sums(o) * (1.0 / RW_N)
    d = o - mean
    var = _head_sums(d * d) * (1.0 / RW_N)
    o = d * lax.rsqrt(var + GN_EPS) * lnw_ref[...] + lnb_ref[...]
    bonus = _head_sums(r * k2 * rk_ref[...]) * v
    y_ref[...] = ((o + bonus) * g).astype(y_ref.dtype)
    _select_finish(pl.program_id(0), sel_ref, sc_ref, **select)


RW_CHUNK = 64

_MIX_PARAMS = ('rw_mu', 'rw_w0', 'rw_w2', 'rw_a0', 'rw_a2', 'rw_g2', 'rw_kk', 'rw_ka')


def _rwkv_prompt_and_select(rw, p, page_table, q_t, ck_view, dec_seq):
    rows = rw.shape[0]
    n_steps = rows // RW_CHUNK
    n_batch, n_pages = page_table.shape
    n_waves = n_batch * (n_pages // WAVE)
    waves_per_step, rem = divmod(n_waves, n_steps)
    assert rem == 0 and waves_per_step >= 1
    steps_per_b = n_steps // n_batch
    rowspec = lambda w: pl.BlockSpec((RW_CHUNK, w), lambda c, pt: (c, 0))
    tail = pl.BlockSpec((SUBLANES, SHIFT_W), lambda c, pt: (jnp.maximum(c * (RW_CHUNK // SUBLANES) - 1, 0), 0))
    const = lambda a: pl.BlockSpec(a.shape, lambda c, pt: (0,) * a.ndim)
    params = [p[n] for n in _MIX_PARAMS] + [p['rw_rk'], p['rw_ln_w'], p['rw_ln_b']]
    select = dict(n_pages=n_pages, n_batch=n_batch, dec_seq=dec_seq, waves_per_step=waves_per_step)
    return pl.pallas_call(
        functools.partial(_rwkv_kernel, chunk=RW_CHUNK, select=select),
        grid_spec=pltpu.PrefetchScalarGridSpec(
            num_scalar_prefetch=1,
            grid=(n_steps,),
            in_specs=[rowspec(SHIFT_W), tail] + [const(a) for a in params]
                     + [pl.BlockSpec((1, HEAD_DIM, LANES), lambda c, pt: (c // steps_per_b, 0, 0)),
                        pl.BlockSpec(memory_space=pl.ANY)],
            out_specs=[rowspec(RW_W), pl.BlockSpec((RW_HEADS, RW_N, RW_N), lambda c, pt: (0, 0, 0)),
                       pl.BlockSpec((1, 8, LANES), lambda c, pt: (c // steps_per_b, 0, 0))],
            scratch_shapes=[pltpu.VMEM((SELECT_SLOTS, WAVE, N_HEADS, HEAD_DIM, PAGE_SIZE), F32),
                            pltpu.SemaphoreType.DMA((SELECT_SLOTS,)),
                            pltpu.VMEM((n_pages // PAGES_PER_BLOCK, LANES), F32)],
        ),
        out_shape=[jax.ShapeDtypeStruct((rows, RW_W), BF16),
                   jax.ShapeDtypeStruct((RW_HEADS, RW_N, RW_N), F32),
                   jax.ShapeDtypeStruct((n_batch, 8, LANES), jnp.int32)],
        compiler_params=pltpu.CompilerParams(dimension_semantics=("arbitrary",), vmem_limit_bytes=VMEM_LIMIT),
        name="rwkv7_prompt_select",
    )(page_table, rw, rw, *params, q_t, ck_view)


def _rwkv_mix_kernel(rw_ref, prev_ref, mu_ref, w0_ref, w2_ref, a0_ref, a2_ref, g2_ref, kk_ref, ka_ref,
                     r_ref, w_ref, k_ref, v_ref, a_ref, b_ref, g_ref):
    r, logw, k2, v, a_vec, b_vec, g = _rwkv_mix(rw_ref[...], prev_ref[...], mu_ref, w0_ref, w2_ref, a0_ref, a2_ref,
                                                g2_ref, kk_ref, ka_ref)
    r_ref[...] = r
    w_ref[...] = jnp.exp(logw)
    k_ref[...] = k2
    v_ref[...] = v
    a_ref[...] = a_vec
    b_ref[...] = b_vec
    g_ref[...] = g


def _rwkv_mix_call(rw, prev, p):
    rows = rw.shape[0]
    rowspec = lambda w: pl.BlockSpec((rows, w), lambda i: (0, 0))
    const = lambda a: pl.BlockSpec(a.shape, lambda i: (0,) * a.ndim)
    params = [p[n] for n in _MIX_PARAMS]
    return pl.pallas_call(
        _rwkv_mix_kernel,
        grid=(1,),
        in_specs=[rowspec(SHIFT_W), rowspec(SHIFT_W)] + [const(a) for a in params],
        out_specs=[rowspec(RW_W)] * 7,
        out_shape=[jax.ShapeDtypeStruct((rows, RW_W), F32)] * 7,
        compiler_params=pltpu.CompilerParams(dimension_semantics=("arbitrary",), vmem_limit_bytes=VMEM_LIMIT),
        name="rwkv7_mix",
    )(rw, prev, *params)


def _rwkv_steps_kernel(r_ref, w_ref, k_ref, v_ref, a_ref, b_ref, g_ref, s_ref, lnw_ref, lnb_ref, rk_ref,
                       y_ref, so_ref, o_scr, *, n_steps):
    def value_row(vi, carry):
        s = s_ref[0, vi]
        for t in range(n_steps):
            sa = jnp.sum(s * a_ref[t, 0], axis=0, keepdims=True)
            s = s * w_ref[t, 0] + sa * b_ref[t, 0] + v_ref[t, 0, pl.ds(vi, 1), :] * k_ref[t, 0]
            o_scr[t, pl.ds(vi, 1), :] = jnp.sum(s * r_ref[t, 0], axis=0, keepdims=True)
        so_ref[0, vi] = s
        return carry

    lax.fori_loop(0, RW_N, value_row, 0)
    for t in range(n_steps):
        o = o_scr[t]
        d = o - jnp.mean(o, axis=0, keepdims=True)
        var = jnp.mean(d * d, axis=0, keepdims=True)
        bonus = jnp.sum(r_ref[t, 0] * k_ref[t, 0] * rk_ref[0], axis=0, keepdims=True)
        o = d * lax.rsqrt(var + GN_EPS) * lnw_ref[0] + lnb_ref[0]
        y_ref[t, 0] = (o + bonus * v_ref[t, 0]) * g_ref[t, 0]


def _rwkv_steps(vecs, state, lnw, lnb, rk):
    n_steps, _, _, n_batch = vecs[0].shape
    vec = pl.BlockSpec((n_steps, 1, RW_N, n_batch), lambda h: (0, h, 0, 0))
    st = pl.BlockSpec((1, RW_N, RW_N, n_batch), lambda h: (h, 0, 0, 0))
    par = pl.BlockSpec((1, RW_N, n_batch), lambda h: (h, 0, 0))
    return pl.pallas_call(
        functools.partial(_rwkv_steps_kernel, n_steps=n_steps),
        grid=(RW_HEADS,),
        in_specs=[vec] * 7 + [st, par, par, par],
        out_specs=[vec, st],
        out_shape=[jax.ShapeDtypeStruct(vecs[0].shape, F32), jax.ShapeDtypeStruct(state.shape, F32)],
        scratch_shapes=[pltpu.VMEM((n_steps, RW_N, n_batch), F32)],
        compiler_params=pltpu.CompilerParams(dimension_semantics=("arbitrary",), vmem_limit_bytes=VMEM_LIMIT),
        name="rwkv7_steps",
    )(*vecs, state, lnw, lnb, rk)


def _rms(x, g):
    return x * lax.rsqrt(jnp.mean(x * x, axis=-1, keepdims=True) + NORM_EPS) * g


def _merge_ffn_kernel(x_ref, ya_ref, yr_ref, gate_ref, wa_ref, wr_ref, wo_ref, wu_ref, wd_ref,
                      n1_ref, n2pre_ref, n2_ref, out_ref):
    d = functools.partial(jnp.dot, preferred_element_type=F32)
    merged = gate_ref[:, :D_MODEL] * d(ya_ref[...], wa_ref[...]) + gate_ref[:, D_MODEL:] * d(yr_ref[...], wr_ref[...])
    x1 = x_ref[...] + _rms(d(merged.astype(BF16), wo_ref[...]), n1_ref[...])
    h2 = _rms(x1, n2pre_ref[...]).astype(BF16)
    u = jnp.square(jnp.maximum(d(h2, wu_ref[...]), 0.0))
    out_ref[...] = x1 + _rms(d(u.astype(BF16), wd_ref[...]), n2_ref[...])


def _merge_ffn(x, y_att, y_rw, gates, wa, wr, wo, wu, wd, n1_post, n2_pre, n2_post):
    rows = x.shape[0]
    tm = 256
    row = lambda w: pl.BlockSpec((tm, w), lambda i: (i, 0))
    const = lambda a: pl.BlockSpec(a.shape, lambda i: (0, 0), pipeline_mode=pl.Buffered(1))
    consts = [wa, wr, wo, wu, wd, n1_post, n2_pre, n2_post]
    return pl.pallas_call(
        _merge_ffn_kernel,
        grid=(rows // tm,),
        in_specs=[row(D_MODEL), row(ATT_W), row(RW_W), row(GATE_W)] + [const(a) for a in consts],
        out_specs=row(D_MODEL),
        out_shape=jax.ShapeDtypeStruct((rows, D_MODEL), F32),
        compiler_params=pltpu.CompilerParams(dimension_semantics=("arbitrary",), vmem_limit_bytes=VMEM_LIMIT),
        name="merge_ffn",
    )(x, y_att, y_rw, gates, *consts)


PAGES_PER_BLOCK = MOBA_BLOCK // PAGE_SIZE
WAVE = 16
SELECT_SLOTS = 4
ATTEND_SLOTS = 4
SEL_BITS = 5


def _select_step(step, pt_ref, qt_ref, ck_ref, buf, sem, sc_ref, *, n_pages, n_batch, dec_seq, waves_per_step):
    waves_per_b = n_pages // WAVE
    n_waves = n_batch * waves_per_b
    assert SELECT_SLOTS >= 2 * waves_per_step and waves_per_b % waves_per_step == 0

    def copies(g):
        bb, w, slot = g // waves_per_b, g % waves_per_b, g % SELECT_SLOTS
        return [pltpu.make_async_copy(ck_ref.at[pt_ref[bb, w * WAVE + i]], buf.at[slot, i], sem.at[slot])
                for i in range(WAVE)]

    @pl.when(step == 0)
    def _():
        for g in range(waves_per_step):
            for cp in copies(g):
                cp.start()

    first = step * waves_per_step
    for k in range(waves_per_step):
        @pl.when(first + waves_per_step + k < n_waves)
        def _():
            for cp in copies(first + waves_per_step + k):
                cp.start()

    for k in range(waves_per_step):
        for cp in copies(first + k):
            cp.wait()

    qt = qt_ref[0]
    lane = lax.broadcasted_iota(jnp.int32, (HEAD_DIM, LANES), 1)
    for k in range(waves_per_step):
        g = first + k
        slot = g % SELECT_SLOTS
        for i in range(0, WAVE, PAGES_PER_BLOCK):
            msel = jnp.zeros((HEAD_DIM, LANES), F32)
            for hh in range(N_HEADS):
                t = buf[slot, i, hh]
                for j in range(1, PAGES_PER_BLOCK):
                    t = t + buf[slot, i + j, hh]
                ksum = jnp.sum(t, axis=1, keepdims=True)
                msel = jnp.where(lane // dec_seq == hh, ksum, msel)
            n = ((g % waves_per_b) * WAVE + i) // PAGES_PER_BLOCK
            sc_ref[pl.ds(n, 1), :] = jnp.sum(msel * qt, axis=0, keepdims=True) * (1.0 / MOBA_BLOCK)


def _select_finish(step, sel_ref, sc_ref, *, n_pages, n_batch, dec_seq, waves_per_step):
    steps_per_b = (n_pages // WAVE) // waves_per_step
    n_full = n_pages // PAGES_PER_BLOCK

    @pl.when(step % steps_per_b == steps_per_b - 1)
    def _():
        sc = sc_ref[...]
        blk = lax.broadcasted_iota(jnp.int32, (n_full, LANES), 0)
        row = lax.broadcasted_iota(jnp.int32, (8, LANES), 0)
        out = jnp.zeros((8, LANES), jnp.int32)
        for r in range(MOBA_TOPK):
            m = jnp.max(sc, axis=0, keepdims=True)
            idx = jnp.min(jnp.where(sc == m, blk, n_full), axis=0, keepdims=True)
            out = jnp.where(row == r, idx, out)
            sc = jnp.where(blk == idx, -jnp.inf, sc)
        sel_ref[0] = out


def _sample_attend_kernel(pt_ref, sel_ref, wq_ref, kn_ref, vn_ref, ck_ref, cv_ref, o_ref, kbuf, vbuf, sem,
                          *, n_batch, dec_seq):
    b = pl.program_id(0)
    h = pl.program_id(1)
    seg = MOBA_TOPK * MOBA_BLOCK

    def copies(g):
        bb, hh, slot = g // N_HEADS, g % N_HEADS, g % ATTEND_SLOTS
        cps = []
        for qi in range(dec_seq):
            packed = sel_ref[bb, hh * dec_seq + qi]
            for r in range(MOBA_TOPK):
                blk = (packed >> (SEL_BITS * r)) & ((1 << SEL_BITS) - 1)
                for pg in range(PAGES_PER_BLOCK):
                    page = pt_ref[bb, blk * PAGES_PER_BLOCK + pg]
                    t = (qi * MOBA_TOPK + r) * PAGES_PER_BLOCK + pg
                    dst = pl.ds(t * PAGE_SIZE, PAGE_SIZE)
                    cps.append(pltpu.make_async_copy(ck_ref.at[page, hh], kbuf.at[slot, :, dst], sem.at[slot, 0]))
                    cps.append(pltpu.make_async_copy(cv_ref.at[page, hh], vbuf.at[slot, :, dst], sem.at[slot, 1]))
        return cps

    g = b * N_HEADS + h
    slot = g % ATTEND_SLOTS

    @pl.when(g == 0)
    def _():
        for g0 in range(ATTEND_SLOTS - 1):
            for cp in copies(g0):
                cp.start()

    @pl.when(g + ATTEND_SLOTS - 1 < n_batch * N_HEADS)
    def _():
        for cp in copies(g + ATTEND_SLOTS - 1):
            cp.start()

    for cp in copies(g):
        cp.wait()

    wq = wq_ref[0]
    kn = kn_ref[0]
    vn = vn_ref[0]
    lane_d = lax.broadcasted_iota(jnp.int32, (HEAD_DIM, LANES), 1)
    lane_1 = lax.broadcasted_iota(jnp.int32, (1, LANES), 1)
    @pl.when(h == 0)
    def _():
        o_ref[...] = jnp.zeros_like(o_ref)

    out = o_ref[0]
    for qi in range(dec_seq):
        me = h * dec_seq + qi
        qcol = jnp.sum(jnp.where(lane_d == me, wq, 0.0), axis=1, keepdims=True)
        keys = slice(qi * seg, (qi + 1) * seg)
        s = jnp.sum(kbuf[slot, :, keys] * qcol, axis=0, keepdims=True)
        s_own = jnp.sum(kn * qcol, axis=0, keepdims=True)
        own = jnp.logical_and(lane_1 // dec_seq == h, lane_1 % dec_seq <= qi)
        s_own = jnp.where(own, s_own, NEG_INF)
        m = jnp.maximum(jnp.max(s, axis=1, keepdims=True), jnp.max(s_own, axis=1, keepdims=True))
        p = jnp.exp(s - m)
        p_own = jnp.exp(s_own - m)
        l = jnp.sum(p, axis=1, keepdims=True) + jnp.sum(p_own, axis=1, keepdims=True)
        o = jnp.sum(vbuf[slot, :, keys] * p, axis=1, keepdims=True) + jnp.sum(vn * p_own, axis=1, keepdims=True)
        out = jnp.where(lane_d == me, o * (1.0 / l), out)
    o_ref[0] = out


def _sample_attend(page_table, sel_packed, wq, kn_t, vn_t, ck_view, cv_view, dec_seq):
    n_batch = page_table.shape[0]
    n_keys = dec_seq * MOBA_TOPK * MOBA_BLOCK
    nblk = pl.BlockSpec((1, HEAD_DIM, LANES), lambda b, h, pt, sel: (b, 0, 0))
    return pl.pallas_call(
        functools.partial(_sample_attend_kernel, n_batch=n_batch, dec_seq=dec_seq),
        grid_spec=pltpu.PrefetchScalarGridSpec(
            num_scalar_prefetch=2,
            grid=(n_batch, N_HEADS),
            in_specs=[nblk, nblk, nblk, pl.BlockSpec(memory_space=pl.ANY), pl.BlockSpec(memory_space=pl.ANY)],
            out_specs=nblk,
            scratch_shapes=[pltpu.VMEM((ATTEND_SLOTS, HEAD_DIM, n_keys), F32),
                            pltpu.VMEM((ATTEND_SLOTS, HEAD_DIM, n_keys), F32),
                            pltpu.SemaphoreType.DMA((ATTEND_SLOTS, 2))],
        ),
        out_shape=jax.ShapeDtypeStruct((n_batch, HEAD_DIM, LANES), F32),
        compiler_params=pltpu.CompilerParams(dimension_semantics=("arbitrary", "arbitrary"),
                                             vmem_limit_bytes=VMEM_LIMIT),
        name="sample_attend",
    )(page_table, sel_packed, wq, kn_t, vn_t, ck_view, cv_view)


def kernel(x_prompt, x_sample, cache_k, cache_v, state_wkv, state_shift, page_table, n1_pre, n1_post, n2_pre, n2_post, w_in, rw_mu, rw_w0, rw_w2, rw_a0, rw_a2, rw_g2, rw_kk, rw_ka, rw_rk, rw_ln_w, rw_ln_b, w_att_br, w_rw_br, w_out, w_up, w_down):
    _, seq, _ = x_prompt.shape
    n_batch, dec_seq, _ = x_sample.shape
    n_pages = page_table.shape[1]
    past = n_pages * PAGE_SIZE
    assert past % MOBA_BLOCK == 0 and n_pages // PAGES_PER_BLOCK >= MOBA_TOPK
    assert n_pages // PAGES_PER_BLOCK <= (1 << SEL_BITS) and x_prompt.shape[0] == 1 and n1_pre.shape[0] == 1
    l = 0
    row2 = lambda a: a[l].reshape(1, -1)
    rwp = dict(rw_mu=row2(rw_mu), rw_w0=row2(rw_w0), rw_w2=rw_w2[l], rw_a0=row2(rw_a0), rw_a2=rw_a2[l],
               rw_g2=rw_g2[l], rw_kk=row2(rw_kk), rw_ka=row2(rw_ka), rw_rk=row2(rw_rk),
               rw_ln_w=row2(rw_ln_w), rw_ln_b=row2(rw_ln_b))
    w_in_b = w_in[l].astype(BF16)
    ffn_w = [w.astype(BF16) for w in (w_att_br[l], w_rw_br[l], w_out[l], w_up[l], w_down[l])]
    norms = [row2(n1_post), row2(n2_pre), row2(n2_post)]
    heads_t = lambda t: t.reshape(N_HEADS, HEAD_DIM, -1)

    xs = x_sample.reshape(n_batch * dec_seq, D_MODEL)
    cos_s, sin_s = _rope_tables(jnp.tile(past + jnp.arange(dec_seq), n_batch))
    qs, _, _, kt_s, vt_s, rw_s, gate_s = _project(xs, cos_s, sin_s, row2(n1_pre), w_in_b, moba_select=False)
    kn_t = heads_t(kt_s).reshape(N_HEADS, HEAD_DIM, n_batch, dec_seq)
    vn_t = heads_t(vt_s).reshape(N_HEADS, HEAD_DIM, n_batch, dec_seq)
    q4 = qs.reshape(n_batch, dec_seq, N_HEADS, HEAD_DIM)
    q_t = q4.transpose(0, 3, 2, 1).reshape(n_batch, HEAD_DIM, N_HEADS * dec_seq)
    q_t = jnp.pad(q_t, ((0, 0), (0, 0), (0, LANES - N_HEADS * dec_seq)))
    ck_view = jnp.swapaxes(cache_k[:, l], -1, -2)
    cv_view = jnp.swapaxes(cache_v[:, l], -1, -2)

    xp = x_prompt[0]
    cos, sin = _rope_tables(jnp.arange(seq))
    q, kb, vtb, kt, vt, rw, gate, blocks_sel = _project(xp, cos, sin, row2(n1_pre), w_in_b, moba_select=True)
    n_blk = seq // MOBA_BLOCK
    y_att = _moba_prompt(q, kb.reshape(n_blk, MOBA_BLOCK, ATT_W), vtb, blocks_sel)
    y_rw, wkv_p, sel = _rwkv_prompt_and_select(rw, rwp, page_table, q_t, ck_view, dec_seq)
    y_prompt = _merge_ffn(xp, y_att, y_rw, gate, *ffn_w, *norms)
    k_prompt = heads_t(kt).transpose(0, 2, 1)[None, None]
    v_prompt = heads_t(vt).transpose(0, 2, 1)[None, None]

    sel = sel[:, :MOBA_TOPK, :N_HEADS * dec_seq]
    sel_packed = sel[:, 0] | (sel[:, 1] << SEL_BITS) | (sel[:, 2] << (2 * SEL_BITS))
    pad_new = lambda t: jnp.pad(t.transpose(2, 1, 0, 3).reshape(n_batch, HEAD_DIM, N_HEADS * dec_seq),
                                ((0, 0), (0, 0), (0, LANES - N_HEADS * dec_seq)))
    o_s = _sample_attend(page_table, sel_packed, q_t * (HEAD_DIM ** -0.5), pad_new(kn_t), pad_new(vn_t),
                         ck_view, cv_view, dec_seq)
    o_s = o_s[:, :, :N_HEADS * dec_seq].reshape(n_batch, HEAD_DIM, N_HEADS, dec_seq)
    y_att_s = o_s.transpose(0, 3, 2, 1).reshape(n_batch * dec_seq, ATT_W).astype(BF16)

    rw_s3 = rw_s.reshape(n_batch, dec_seq, SHIFT_W)
    prev_s = jnp.concatenate([state_shift[:, l][:, None], rw_s3[:, :-1]], axis=1).reshape(n_batch * dec_seq, SHIFT_W)
    to_steps = lambda t: t.reshape(n_batch, dec_seq, RW_HEADS, RW_N).transpose(1, 2, 3, 0)
    vecs = [to_steps(t) for t in _rwkv_mix_call(rw_s, prev_s, rwp)]
    on_lanes = lambda a: jnp.broadcast_to(a[l].reshape(RW_HEADS, RW_N, 1), (RW_HEADS, RW_N, n_batch))
    y_steps, wkv_s = _rwkv_steps(vecs, state_wkv[:, l].transpose(1, 2, 3, 0),
                                 on_lanes(rw_ln_w), on_lanes(rw_ln_b), on_lanes(rw_rk))
    y_rw_s = y_steps.transpose(3, 0, 1, 2).reshape(n_batch * dec_seq, RW_W).astype(BF16)
    y_sample = _merge_ffn(xs, y_att_s, y_rw_s, gate_s, *ffn_w, *norms)
    k_sample = kn_t.transpose(2, 0, 3, 1)[:, None]
    v_sample = vn_t.transpose(2, 0, 3, 1)[:, None]

    return (y_prompt[None], y_sample.reshape(n_batch, dec_seq, D_MODEL), k_prompt, v_prompt, k_sample, v_sample,
            wkv_p[None, None], wkv_s.transpose(3, 0, 1, 2)[:, None], rw[-1][None, None], rw_s3[:, -1][:, None])
```

```python
import functools
import math

import jax
import jax.numpy as jnp
from jax import lax
from jax.experimental import pallas as pl
from jax.experimental.pallas import tpu as pltpu

F32 = jnp.float32
BF16 = jnp.bfloat16

D_MODEL = 1024
N_HEADS = 8
HEAD_DIM = 64
ATT_W = N_HEADS * HEAD_DIM
RW_HEADS = 8
RW_N = 64
RW_W = RW_HEADS * RW_N
DECAY_LORA = 64
AAA_LORA = 64
GATE_LORA = 128
D_FF = 4 * D_MODEL
MOBA_BLOCK = 256
MOBA_TOPK = 3
Q_BLOCK = 128
PAGE_SIZE = 128
ROPE_THETA = 10000.0
NORM_EPS = 1e-6
GN_EPS = 64e-5
NEG_INF = -1e30
LOG2_E = 1.4426950408889634
SHIFT_W = 3 * RW_W + DECAY_LORA + AAA_LORA + GATE_LORA
GATE_W = 2 * D_MODEL
IN_W = 3 * ATT_W + SHIFT_W + GATE_W

LANES = 128
SUBLANES = 8
ONES_ROWS = 16
MOBA_GROUP = 4
MOBA_COLS = (MOBA_BLOCK // Q_BLOCK) * 2 * Q_BLOCK
VMEM_LIMIT = 56 * 1024 * 1024

NT_DIMS = (((1,), (1,)), ((), ()))
TN_DIMS = (((0,), (0,)), ((), ()))


def _split2(x):
    hi = x.astype(BF16)
    lo = (x - hi.astype(F32)).astype(BF16)
    return hi, lo


def _dot3(a, b, dims=(((1,), (0,)), ((), ()))):
    ah, al = _split2(a)
    bh, bl = _split2(b)
    d = functools.partial(lax.dot_general, dimension_numbers=dims, preferred_element_type=F32)
    return d(ah, bh) + (d(ah, bl) + d(al, bh))


def _split3(x):
    x1 = x.astype(BF16)
    r1 = x - x1.astype(F32)
    x2 = r1.astype(BF16)
    x3 = (r1 - x2.astype(F32)).astype(BF16)
    return x1, x2, x3


def _dot_exact_lhs(mask, b):
    a = mask.astype(BF16)
    b1, b2, b3 = _split3(b)
    d = functools.partial(jnp.dot, preferred_element_type=F32)
    return d(a, b1) + (d(a, b2) + d(a, b3))


def _head_sums(x):
    li = lax.broadcasted_iota(jnp.int32, (LANES, LANES), 0) // RW_N
    lj = lax.broadcasted_iota(jnp.int32, (LANES, LANES), 1) // RW_N
    ones = (li == lj).astype(BF16)
    d = functools.partial(jnp.dot, preferred_element_type=F32)
    out = []
    for c in range(RW_W // LANES):
        hi, lo = _split2(x[:, c * LANES:(c + 1) * LANES])
        out.append(d(hi, ones) + d(lo, ones))
    return jnp.concatenate(out, axis=1)


def _sigmoid(x):
    return 1.0 / (1.0 + jnp.exp(-x))


def _proj_kernel(x_ref, cos_ref, sin_ref, g_ref, w_ref,
                 q_ref, kb_ref, vtb_ref, kt_ref, vt_ref, rw_ref, gate_ref, kmean_ref, *, tm):
    x = x_ref[...]
    ms = jnp.mean(x * x, axis=-1, keepdims=True)
    h = (x * lax.rsqrt(ms + NORM_EPS) * g_ref[...]).astype(BF16)
    cos = cos_ref[...]
    sin = sin_ref[...]
    lane = lax.broadcasted_iota(jnp.int32, (tm, LANES), 1)
    first_half = (lane % HEAD_DIM) < (HEAD_DIM // 2)

    def rope(p):
        sw = jnp.where(first_half, pltpu.roll(p, LANES - HEAD_DIM // 2, 1), pltpu.roll(p, HEAD_DIM // 2, 1))
        return p * cos + sw * sin

    def proj(lo, width):
        return jnp.dot(h, w_ref[:, lo:lo + width], preferred_element_type=F32)

    pq = proj(0, ATT_W)
    for c in range(ATT_W // LANES):
        q_ref[:, c * LANES:(c + 1) * LANES] = rope(pq[:, c * LANES:(c + 1) * LANES])
    pk = proj(ATT_W, ATT_W)
    for c in range(ATT_W // LANES):
        cols = slice(c * LANES, (c + 1) * LANES)
        kr = rope(pk[:, cols])
        kb_ref[:, cols] = kr.astype(BF16)
        kt_ref[cols, :] = kr.T
        kmean_ref[0, :, cols] = jnp.sum(kr, axis=0, keepdims=True) * (1.0 / tm)
    pv = proj(2 * ATT_W, ATT_W)
    for c in range(ATT_W // LANES):
        cols = slice(c * LANES, (c + 1) * LANES)
        vt = pv[:, cols].T
        vt_ref[cols, :] = vt
        vtb_ref[0, cols, :] = vt.astype(BF16)
    rw_ref[...] = proj(3 * ATT_W, SHIFT_W)
    gate_ref[...] = _sigmoid(proj(3 * ATT_W + SHIFT_W, GATE_W))


def _rope_tables(pos):
    half = HEAD_DIM // 2
    inv = ROPE_THETA ** (-jnp.arange(half, dtype=F32) / half)
    ang = pos.astype(F32)[:, None] * inv[None, :]
    cos = jnp.cos(ang)
    sin = jnp.sin(ang)
    return jnp.tile(jnp.concatenate([cos, cos], axis=1), (1, 2)), jnp.tile(jnp.concatenate([-sin, sin], axis=1), (1, 2))


def _project(x, cos, sin, g, w_in_bf16):
    rows = x.shape[0]
    tm = MOBA_BLOCK
    n = rows // tm
    row = lambda w: pl.BlockSpec((tm, w), lambda i: (i, 0))
    const = lambda a, b: pl.BlockSpec((a, b), lambda i: (0, 0))
    col = pl.BlockSpec((ATT_W, tm), lambda i: (0, i))
    return pl.pallas_call(
        functools.partial(_proj_kernel, tm=tm),
        grid=(n,),
        in_specs=[row(D_MODEL), row(LANES), row(LANES), const(1, D_MODEL), const(D_MODEL, IN_W)],
        out_specs=[row(ATT_W), row(ATT_W), pl.BlockSpec((1, ATT_W, tm), lambda i: (i, 0, 0)), col, col,
                   row(SHIFT_W), row(GATE_W), pl.BlockSpec((1, 1, ATT_W), lambda i: (i, 0, 0))],
        out_shape=[
            jax.ShapeDtypeStruct((rows, ATT_W), F32),
            jax.ShapeDtypeStruct((rows, ATT_W), BF16),
            jax.ShapeDtypeStruct((n, ATT_W, tm), BF16),
            jax.ShapeDtypeStruct((ATT_W, rows), F32),
            jax.ShapeDtypeStruct((ATT_W, rows), F32),
            jax.ShapeDtypeStruct((rows, SHIFT_W), F32),
            jax.ShapeDtypeStruct((rows, GATE_W), F32),
            jax.ShapeDtypeStruct((n, 1, ATT_W), F32),
        ],
        compiler_params=pltpu.CompilerParams(dimension_semantics=("arbitrary",), vmem_limit_bytes=VMEM_LIMIT),
        name="project",
    )(x, cos, sin, g, w_in_bf16)


def _moba_prompt_kernel(q_ref, k_ref, vt_ref, kmean_ref, y_ref, sel_ref, sa_ref, sb_ref, *, n_blk):
    ob = pl.program_id(1)
    lane = lax.broadcasted_iota(jnp.int32, (Q_BLOCK, LANES), 1)
    kmean = kmean_ref[...]
    blk = lax.broadcasted_iota(jnp.int32, (n_blk, Q_BLOCK), 0)
    n_ch = MOBA_BLOCK // Q_BLOCK
    qs = {}
    for ch in range(n_ch):
        q = q_ref[ch * Q_BLOCK:(ch + 1) * Q_BLOCK, :]
        for hh in range(2):
            qh = jnp.where((lane // HEAD_DIM) == hh, q, 0.0)
            qs[hh * n_ch + ch] = (qh * (HEAD_DIM ** -0.5 * LOG2_E)).astype(BF16)
            sb = _dot3(kmean, qh, NT_DIMS)
            sb = jnp.where(blk < ob, sb, NEG_INF)
            sel = jnp.zeros((n_blk, Q_BLOCK), jnp.bool_)
            for _ in range(MOBA_TOPK):
                m = jnp.max(sb, axis=0, keepdims=True)
                idx = jnp.min(jnp.where(sb == m, blk, n_blk), axis=0, keepdims=True)
                pick = blk == idx
                sel = jnp.logical_or(sel, pick)
                sb = jnp.where(pick, -jnp.inf, sb)
            sel = jnp.logical_and(sel, blk < ob)
            col0 = (hh * n_ch + ch) * Q_BLOCK
            sel_ref[:, col0:col0 + Q_BLOCK] = sel.astype(F32)
    qs = jnp.concatenate([qs[hh * n_ch + ch] for hh in range(2) for ch in range(n_ch)], axis=0)

    def scores(j):
        return lax.dot_general(k_ref[j], qs, NT_DIMS, preferred_element_type=F32)

    s = scores(ob)
    key = lax.broadcasted_iota(jnp.int32, (MOBA_BLOCK, MOBA_COLS), 0)
    col = lax.broadcasted_iota(jnp.int32, (MOBA_BLOCK, MOBA_COLS), 1)
    s = jnp.where(key <= (col // Q_BLOCK % n_ch) * Q_BLOCK + col % Q_BLOCK, s, NEG_INF)
    m0 = jnp.max(s, axis=0, keepdims=True)
    ones_rows = jnp.ones((ONES_ROWS, MOBA_BLOCK), BF16)

    def pv(j, p):
        vt = vt_ref[j]
        pb = p.astype(BF16)
        per_head = MOBA_COLS // 2
        return jnp.concatenate(
            [jnp.dot(jnp.concatenate([vt[hh * HEAD_DIM:(hh + 1) * HEAD_DIM], ones_rows], axis=0),
                     pb[:, hh * per_head:(hh + 1) * per_head], preferred_element_type=F32) for hh in range(2)],
            axis=1)

    acc0 = pv(ob, jnp.exp2(s - m0))

    def block(g, t):
        return jnp.minimum(MOBA_GROUP * g + t, n_blk - 1)

    def masked_scores(j):
        return jnp.where(sel_ref[pl.ds(j, 1), :] > 0.0, scores(j), NEG_INF)

    def group_step(g, cur_ref, nxt_ref, m_cur, m_acc, acc):
        mx = jnp.full((1, MOBA_COLS), NEG_INF, F32)
        tot = None
        for t in range(MOBA_GROUP):
            sn = masked_scores(block(g + 1, t))
            nxt_ref[t] = sn
            mx = jnp.maximum(mx, jnp.max(sn, axis=0, keepdims=True))
            part = pv(block(g, t), jnp.exp2(cur_ref[t] - m_cur))
            tot = part if tot is None else tot + part
        return jnp.maximum(m_cur, mx), m_cur, acc * jnp.exp2(m_acc - m_cur) + tot

    mx = m0
    for t in range(MOBA_GROUP):
        s = masked_scores(block(0, t))
        sa_ref[t] = s
        mx = jnp.maximum(mx, jnp.max(s, axis=0, keepdims=True))

    def body(i, carry):
        m_cur, m_acc, acc = group_step(2 * i, sa_ref, sb_ref, *carry)
        return group_step(2 * i + 1, sb_ref, sa_ref, m_cur, m_acc, acc)

    n_groups = (ob + MOBA_GROUP - 1) // MOBA_GROUP
    _, _, acc = lax.fori_loop(0, (n_groups + 1) // 2, body, (mx, m0, acc0))
    o = acc[:HEAD_DIM] * (1.0 / acc[HEAD_DIM:HEAD_DIM + 1])
    for ch in range(n_ch):
        ot = jnp.concatenate([o[:, (hh * n_ch + ch) * Q_BLOCK:(hh * n_ch + ch + 1) * Q_BLOCK] for hh in range(2)],
                             axis=0)
        y_ref[ch * Q_BLOCK:(ch + 1) * Q_BLOCK, :] = ot.T.astype(y_ref.dtype)


def _moba_prompt(q, k_blocks, vt_blocks, kmean):
    s = q.shape[0]
    n_blk = k_blocks.shape[0]
    return pl.pallas_call(
        functools.partial(_moba_prompt_kernel, n_blk=n_blk),
        grid=(ATT_W // LANES, n_blk),
        in_specs=[
            pl.BlockSpec((MOBA_BLOCK, LANES), lambda hp, c: (c, hp)),
            pl.BlockSpec((n_blk, MOBA_BLOCK, LANES), lambda hp, c: (0, 0, hp)),
            pl.BlockSpec((n_blk, LANES, MOBA_BLOCK), lambda hp, c: (0, hp, 0)),
            pl.BlockSpec((n_blk, LANES), lambda hp, c: (0, hp)),
        ],
        out_specs=pl.BlockSpec((MOBA_BLOCK, LANES), lambda hp, c: (c, hp)),
        out_shape=jax.ShapeDtypeStruct((s, ATT_W), BF16),
        scratch_shapes=[pltpu.VMEM((n_blk, MOBA_COLS), F32),
                        pltpu.VMEM((MOBA_GROUP, MOBA_BLOCK, MOBA_COLS), F32),
                        pltpu.VMEM((MOBA_GROUP, MOBA_BLOCK, MOBA_COLS), F32)],
        compiler_params=pltpu.CompilerParams(dimension_semantics=("arbitrary", "arbitrary"),
                                             vmem_limit_bytes=VMEM_LIMIT),
        name="moba_prompt",
    )(q, k_blocks, vt_blocks, kmean)


def _rwkv_mix(rw, prev, mu_ref, w0_ref, w2_ref, a0_ref, a2_ref, g2_ref, kk_ref, ka_ref):
    m = rw + (prev - rw) * mu_ref[...]
    r = m[:, 0:RW_W]
    k = m[:, RW_W:2 * RW_W]
    v = m[:, 2 * RW_W:3 * RW_W]
    o1 = 3 * RW_W
    wd = m[:, o1:o1 + DECAY_LORA]
    ad = m[:, o1 + DECAY_LORA:o1 + DECAY_LORA + AAA_LORA]
    gd = m[:, o1 + DECAY_LORA + AAA_LORA:]
    bdot = lambda a, b: jnp.dot(a.astype(BF16), b.astype(BF16), preferred_element_type=F32)
    w_raw = w0_ref[...] + bdot(jnp.tanh(wd), w2_ref[...])
    logw = -math.exp(-0.5) * _sigmoid(w_raw)
    a_sig = _sigmoid(a0_ref[...] + bdot(ad, a2_ref[...]))
    g = bdot(_sigmoid(gd), g2_ref[...])

    kk = k * kk_ref[...]
    kk_norm = jnp.sqrt(_head_sums(kk * kk))
    kk = kk / jnp.maximum(kk_norm, 1e-12)
    k2 = k * (1.0 + (a_sig - 1.0) * ka_ref[...])
    return r, logw, k2, v, -kk, kk * a_sig, g


def _rwkv_kernel(pt_ref, rw_ref, tail_ref, mu_ref, w0_ref, w2_ref, a0_ref, a2_ref, g2_ref, kk_ref, ka_ref,
                 rk_ref, lnw_ref, lnb_ref, qt_ref, ck_ref, y_ref, s_ref, sel_ref, buf, sem, sc_ref,
                 *, chunk, select):
    @pl.when(pl.program_id(0) == 0)
    def _():
        s_ref[...] = jnp.zeros_like(s_ref)

    _select_step(pl.program_id(0), pt_ref, qt_ref, ck_ref, buf, sem, sc_ref, **select)

    rw = rw_ref[...]
    before = jnp.where(pl.program_id(0) == 0, 0.0, tail_ref[SUBLANES - 1:SUBLANES, :])
    first_row = lax.broadcasted_iota(jnp.int32, rw.shape, 0) == 0
    prev = jnp.where(first_row, before, pltpu.roll(rw, 1, 0))
    r, logw, k2, v, a_vec, b_vec, g = _rwkv_mix(rw, prev, mu_ref, w0_ref, w2_ref, a0_ref, a2_ref,
                                                g2_ref, kk_ref, ka_ref)
    ti = lax.broadcasted_iota(jnp.int32, (chunk, chunk), 0)
    tj = lax.broadcasted_iota(jnp.int32, (chunk, chunk), 1)
    lower_incl = tj <= ti
    lower_strict = tj < ti
    cs = _dot_exact_lhs(lower_incl.astype(F32), logw)
    p_incl = jnp.exp(cs)
    p_inv = jnp.exp(-cs)
    a_t = a_vec * jnp.exp(cs - logw)
    r_t = r * p_incl
    b_t = b_vec * p_inv
    k_t = k2 * p_inv
    p_last = p_incl[chunk - 1:chunk, :]
    eye = (ti == tj).astype(F32)

    heads = range(RW_HEADS)
    sl = [slice(h * RW_N, (h + 1) * RW_N) for h in heads]
    ar = [jnp.concatenate([a_t[:, s], r_t[:, s]], axis=0) for s in sl]
    bk = [jnp.concatenate([b_t[:, s], k_t[:, s]], axis=0) for s in sl]
    gram = [_dot3(ar[h], bk[h], NT_DIMS) for h in heads]
    s0 = [s_ref[h] for h in heads]
    ars = [_dot3(ar[h], s0[h], NT_DIMS) for h in heads]
    l_ak_v = [_dot3(jnp.where(lower_strict, gram[h][:chunk, chunk:], 0.0), v[:, sl[h]]) for h in heads]
    l_ab = [jnp.where(lower_strict, gram[h][:chunk, :chunk], 0.0) for h in heads]
    x = [eye + l_ab[h] for h in heads]
    pw = [_dot3(l_ab[h], l_ab[h]) for h in heads]
    n_factors = int(math.log2(chunk))
    for it in range(1, n_factors):
        if it + 1 < n_factors:
            prod = [_dot3(pw[h], jnp.concatenate([pw[h], x[h]], axis=1)) for h in heads]
            pw = [prod[h][:, :chunk] for h in heads]
            x = [x[h] + prod[h][:, chunk:] for h in heads]
        else:
            x = [x[h] + _dot3(pw[h], x[h]) for h in heads]
    u = [_dot3(x[h], ars[h][:chunk] + l_ak_v[h]) for h in heads]
    uv = [jnp.concatenate([u[h], v[:, sl[h]]], axis=0) for h in heads]
    wide_i = lax.broadcasted_iota(jnp.int32, (chunk, 2 * chunk), 0)
    wide_j = lax.broadcasted_iota(jnp.int32, (chunk, 2 * chunk), 1) % chunk
    m_r = [jnp.where(wide_j <= wide_i, gram[h][chunk:, :], 0.0) for h in heads]
    ys = [ars[h][chunk:] + _dot3(m_r[h], uv[h]) for h in heads]
    for h in heads:
        s_ref[h] = (s0[h] + _dot3(uv[h], bk[h], TN_DIMS)) * p_last[:, sl[h]]
    o = jnp.concatenate(ys, axis=1)

    mean = _head_sums(o) * (1.0 / RW_N)
    d = o - mean
    var = _head_sums(d * d) * (1.0 / RW_N)
    o = d * lax.rsqrt(var + GN_EPS) * lnw_ref[...] + lnb_ref[...]
    bonus = _head_sums(r * k2 * rk_ref[...]) * v
    y_ref[...] = ((o + bonus) * g).astype(y_ref.dtype)
    _select_finish(pl.program_id(0), sel_ref, sc_ref, **select)


RW_CHUNK = 64

_MIX_PARAMS = ('rw_mu', 'rw_w0', 'rw_w2', 'rw_a0', 'rw_a2', 'rw_g2', 'rw_kk', 'rw_ka')


def _rwkv_prompt_and_select(rw, p, page_table, q_t, ck_view, dec_seq):
    rows = rw.shape[0]
    n_steps = rows // RW_CHUNK
    n_batch, n_pages = page_table.shape
    n_waves = n_batch * (n_pages // WAVE)
    waves_per_step, rem = divmod(n_waves, n_steps)
    assert rem == 0 and waves_per_step >= 1
    steps_per_b = n_steps // n_batch
    rowspec = lambda w: pl.BlockSpec((RW_CHUNK, w), lambda c, pt: (c, 0))
    tail = pl.BlockSpec((SUBLANES, SHIFT_W), lambda c, pt: (jnp.maximum(c * (RW_CHUNK // SUBLANES) - 1, 0), 0))
    const = lambda a: pl.BlockSpec(a.shape, lambda c, pt: (0,) * a.ndim)
    params = [p[n] for n in _MIX_PARAMS] + [p['rw_rk'], p['rw_ln_w'], p['rw_ln_b']]
    select = dict(n_pages=n_pages, n_batch=n_batch, dec_seq=dec_seq, waves_per_step=waves_per_step)
    return pl.pallas_call(
        functools.partial(_rwkv_kernel, chunk=RW_CHUNK, select=select),
        grid_spec=pltpu.PrefetchScalarGridSpec(
            num_scalar_prefetch=1,
            grid=(n_steps,),
            in_specs=[rowspec(SHIFT_W), tail] + [const(a) for a in params]
                     + [pl.BlockSpec((1, HEAD_DIM, LANES), lambda c, pt: (c // steps_per_b, 0, 0)),
                        pl.BlockSpec(memory_space=pl.ANY)],
            out_specs=[rowspec(RW_W), pl.BlockSpec((RW_HEADS, RW_N, RW_N), lambda c, pt: (0, 0, 0)),
                       pl.BlockSpec((1, 8, LANES), lambda c, pt: (c // steps_per_b, 0, 0))],
            scratch_shapes=[pltpu.VMEM((SELECT_SLOTS, WAVE, N_HEADS, HEAD_DIM, PAGE_SIZE), F32),
                            pltpu.SemaphoreType.DMA((SELECT_SLOTS,)),
                            pltpu.VMEM((n_pages // PAGES_PER_BLOCK, LANES), F32)],
        ),
        out_shape=[jax.ShapeDtypeStruct((rows, RW_W), BF16),
                   jax.ShapeDtypeStruct((RW_HEADS, RW_N, RW_N), F32),
                   jax.ShapeDtypeStruct((n_batch, 8, LANES), jnp.int32)],
        compiler_params=pltpu.CompilerParams(dimension_semantics=("arbitrary",), vmem_limit_bytes=VMEM_LIMIT),
        name="rwkv7_prompt_select",
    )(page_table, rw, rw, *params, q_t, ck_view)


def _rwkv_mix_kernel(rw_ref, prev_ref, mu_ref, w0_ref, w2_ref, a0_ref, a2_ref, g2_ref, kk_ref, ka_ref,
                     r_ref, w_ref, k_ref, v_ref, a_ref, b_ref, g_ref):
    r, logw, k2, v, a_vec, b_vec, g = _rwkv_mix(rw_ref[...], prev_ref[...], mu_ref, w0_ref, w2_ref, a0_ref, a2_ref,
                                                g2_ref, kk_ref, ka_ref)
    r_ref[...] = r
    w_ref[...] = jnp.exp(logw)
    k_ref[...] = k2
    v_ref[...] = v
    a_ref[...] = a_vec
    b_ref[...] = b_vec
    g_ref[...] = g


def _rwkv_mix_call(rw, prev, p):
    rows = rw.shape[0]
    rowspec = lambda w: pl.BlockSpec((rows, w), lambda i: (0, 0))
    const = lambda a: pl.BlockSpec(a.shape, lambda i: (0,) * a.ndim)
    params = [p[n] for n in _MIX_PARAMS]
    return pl.pallas_call(
        _rwkv_mix_kernel,
        grid=(1,),
        in_specs=[rowspec(SHIFT_W), rowspec(SHIFT_W)] + [const(a) for a in params],
        out_specs=[rowspec(RW_W)] * 7,
        out_shape=[jax.ShapeDtypeStruct((rows, RW_W), F32)] * 7,
        compiler_params=pltpu.CompilerParams(dimension_semantics=("arbitrary",), vmem_limit_bytes=VMEM_LIMIT),
        name="rwkv7_mix",
    )(rw, prev, *params)


def _rwkv_steps_kernel(r_ref, w_ref, k_ref, v_ref, a_ref, b_ref, g_ref, s_ref, lnw_ref, lnb_ref, rk_ref,
                       y_ref, so_ref, o_scr, *, n_steps):
    def value_row(vi, carry):
        s = s_ref[0, vi]
        for t in range(n_steps):
            sa = jnp.sum(s * a_ref[t, 0], axis=0, keepdims=True)
            s = s * w_ref[t, 0] + sa * b_ref[t, 0] + v_ref[t, 0, pl.ds(vi, 1), :] * k_ref[t, 0]
            o_scr[t, pl.ds(vi, 1), :] = jnp.sum(s * r_ref[t, 0], axis=0, keepdims=True)
        so_ref[0, vi] = s
        return carry

    lax.fori_loop(0, RW_N, value_row, 0)
    for t in range(n_steps):
        o = o_scr[t]
        d = o - jnp.mean(o, axis=0, keepdims=True)
        var = jnp.mean(d * d, axis=0, keepdims=True)
        bonus = jnp.sum(r_ref[t, 0] * k_ref[t, 0] * rk_ref[0], axis=0, keepdims=True)
        o = d * lax.rsqrt(var + GN_EPS) * lnw_ref[0] + lnb_ref[0]
        y_ref[t, 0] = (o + bonus * v_ref[t, 0]) * g_ref[t, 0]


def _rwkv_steps(vecs, state, lnw, lnb, rk):
    n_steps, _, _, n_batch = vecs[0].shape
    vec = pl.BlockSpec((n_steps, 1, RW_N, n_batch), lambda h: (0, h, 0, 0))
    st = pl.BlockSpec((1, RW_N, RW_N, n_batch), lambda h: (h, 0, 0, 0))
    par = pl.BlockSpec((1, RW_N, n_batch), lambda h: (h, 0, 0))
    return pl.pallas_call(
        functools.partial(_rwkv_steps_kernel, n_steps=n_steps),
        grid=(RW_HEADS,),
        in_specs=[vec] * 7 + [st, par, par, par],
        out_specs=[vec, st],
        out_shape=[jax.ShapeDtypeStruct(vecs[0].shape, F32), jax.ShapeDtypeStruct(state.shape, F32)],
        scratch_shapes=[pltpu.VMEM((n_steps, RW_N, n_batch), F32)],
        compiler_params=pltpu.CompilerParams(dimension_semantics=("arbitrary",), vmem_limit_bytes=VMEM_LIMIT),
        name="rwkv7_steps",
    )(*vecs, state, lnw, lnb, rk)


def _rms(x, g):
    return x * lax.rsqrt(jnp.mean(x * x, axis=-1, keepdims=True) + NORM_EPS) * g


def _merge_ffn_kernel(x_ref, ya_ref, yr_ref, gate_ref, wa_ref, wr_ref, wo_ref, wu_ref, wd_ref,
                      n1_ref, n2pre_ref, n2_ref, out_ref):
    d = functools.partial(jnp.dot, preferred_element_type=F32)
    merged = gate_ref[:, :D_MODEL] * d(ya_ref[...], wa_ref[...]) + gate_ref[:, D_MODEL:] * d(yr_ref[...], wr_ref[...])
    x1 = x_ref[...] + _rms(d(merged.astype(BF16), wo_ref[...]), n1_ref[...])
    h2 = _rms(x1, n2pre_ref[...]).astype(BF16)
    u = jnp.square(jnp.maximum(d(h2, wu_ref[...]), 0.0))
    out_ref[...] = x1 + _rms(d(u.astype(BF16), wd_ref[...]), n2_ref[...])


def _merge_ffn(x, y_att, y_rw, gates, wa, wr, wo, wu, wd, n1_post, n2_pre, n2_post):
    rows = x.shape[0]
    tm = 256
    row = lambda w: pl.BlockSpec((tm, w), lambda i: (i, 0))
    const = lambda a: pl.BlockSpec(a.shape, lambda i: (0, 0), pipeline_mode=pl.Buffered(1))
    consts = [wa, wr, wo, wu, wd, n1_post, n2_pre, n2_post]
    return pl.pallas_call(
        _merge_ffn_kernel,
        grid=(rows // tm,),
        in_specs=[row(D_MODEL), row(ATT_W), row(RW_W), row(GATE_W)] + [const(a) for a in consts],
        out_specs=row(D_MODEL),
        out_shape=jax.ShapeDtypeStruct((rows, D_MODEL), F32),
        compiler_params=pltpu.CompilerParams(dimension_semantics=("arbitrary",), vmem_limit_bytes=VMEM_LIMIT),
        name="merge_ffn",
    )(x, y_att, y_rw, gates, *consts)


PAGES_PER_BLOCK = MOBA_BLOCK // PAGE_SIZE
WAVE = 16
SELECT_SLOTS = 4
ATTEND_SLOTS = 4
SEL_BITS = 5


def _select_step(step, pt_ref, qt_ref, ck_ref, buf, sem, sc_ref, *, n_pages, n_batch, dec_seq, waves_per_step):
    waves_per_b = n_pages // WAVE
    n_waves = n_batch * waves_per_b
    assert SELECT_SLOTS >= 2 * waves_per_step and waves_per_b % waves_per_step == 0

    def copies(g):
        bb, w, slot = g // waves_per_b, g % waves_per_b, g % SELECT_SLOTS
        return [pltpu.make_async_copy(ck_ref.at[pt_ref[bb, w * WAVE + i]], buf.at[slot, i], sem.at[slot])
                for i in range(WAVE)]

    @pl.when(step == 0)
    def _():
        for g in range(waves_per_step):
            for cp in copies(g):
                cp.start()

    first = step * waves_per_step
    for k in range(waves_per_step):
        @pl.when(first + waves_per_step + k < n_waves)
        def _():
            for cp in copies(first + waves_per_step + k):
                cp.start()

    for k in range(waves_per_step):
        for cp in copies(first + k):
            cp.wait()

    qt = qt_ref[0]
    lane = lax.broadcasted_iota(jnp.int32, (HEAD_DIM, LANES), 1)
    for k in range(waves_per_step):
        g = first + k
        slot = g % SELECT_SLOTS
        for i in range(0, WAVE, PAGES_PER_BLOCK):
            msel = jnp.zeros((HEAD_DIM, LANES), F32)
            for hh in range(N_HEADS):
                t = buf[slot, i, hh]
                for j in range(1, PAGES_PER_BLOCK):
                    t = t + buf[slot, i + j, hh]
                ksum = jnp.sum(t, axis=1, keepdims=True)
                msel = jnp.where(lane // dec_seq == hh, ksum, msel)
            n = ((g % waves_per_b) * WAVE + i) // PAGES_PER_BLOCK
            sc_ref[pl.ds(n, 1), :] = jnp.sum(msel * qt, axis=0, keepdims=True) * (1.0 / MOBA_BLOCK)


def _select_finish(step, sel_ref, sc_ref, *, n_pages, n_batch, dec_seq, waves_per_step):
    steps_per_b = (n_pages // WAVE) // waves_per_step
    n_full = n_pages // PAGES_PER_BLOCK

    @pl.when(step % steps_per_b == steps_per_b - 1)
    def _():
        sc = sc_ref[...]
        blk = lax.broadcasted_iota(jnp.int32, (n_full, LANES), 0)
        row = lax.broadcasted_iota(jnp.int32, (8, LANES), 0)
        out = jnp.zeros((8, LANES), jnp.int32)
        for r in range(MOBA_TOPK):
            m = jnp.max(sc, axis=0, keepdims=True)
            idx = jnp.min(jnp.where(sc == m, blk, n_full), axis=0, keepdims=True)
            out = jnp.where(row == r, idx, out)
            sc = jnp.where(blk == idx, -jnp.inf, sc)
        sel_ref[0] = out


def _sample_attend_kernel(pt_ref, sel_ref, wq_ref, kn_ref, vn_ref, ck_ref, cv_ref, o_ref, kbuf, vbuf, sem,
                          *, n_batch, dec_seq):
    b = pl.program_id(0)
    h = pl.program_id(1)
    seg = MOBA_TOPK * MOBA_BLOCK

    def copies(g):
        bb, hh, slot = g // N_HEADS, g % N_HEADS, g % ATTEND_SLOTS
        cps = []
        for qi in range(dec_seq):
            packed = sel_ref[bb, hh * dec_seq + qi]
            for r in range(MOBA_TOPK):
                blk = (packed >> (SEL_BITS * r)) & ((1 << SEL_BITS) - 1)
                for pg in range(PAGES_PER_BLOCK):
                    page = pt_ref[bb, blk * PAGES_PER_BLOCK + pg]
                    t = (qi * MOBA_TOPK + r) * PAGES_PER_BLOCK + pg
                    dst = pl.ds(t * PAGE_SIZE, PAGE_SIZE)
                    cps.append(pltpu.make_async_copy(ck_ref.at[page, hh], kbuf.at[slot, :, dst], sem.at[slot, 0]))
                    cps.append(pltpu.make_async_copy(cv_ref.at[page, hh], vbuf.at[slot, :, dst], sem.at[slot, 1]))
        return cps

    g = b * N_HEADS + h
    slot = g % ATTEND_SLOTS

    @pl.when(g == 0)
    def _():
        for g0 in range(ATTEND_SLOTS - 1):
            for cp in copies(g0):
                cp.start()

    @pl.when(g + ATTEND_SLOTS - 1 < n_batch * N_HEADS)
    def _():
        for cp in copies(g + ATTEND_SLOTS - 1):
            cp.start()

    for cp in copies(g):
        cp.wait()

    wq = wq_ref[0]
    kn = kn_ref[0]
    vn = vn_ref[0]
    lane_d = lax.broadcasted_iota(jnp.int32, (HEAD_DIM, LANES), 1)
    lane_1 = lax.broadcasted_iota(jnp.int32, (1, LANES), 1)
    @pl.when(h == 0)
    def _():
        o_ref[...] = jnp.zeros_like(o_ref)

    out = o_ref[0]
    for qi in range(dec_seq):
        me = h * dec_seq + qi
        qcol = jnp.sum(jnp.where(lane_d == me, wq, 0.0), axis=1, keepdims=True)
        keys = slice(qi * seg, (qi + 1) * seg)
        s = jnp.sum(kbuf[slot, :, keys] * qcol, axis=0, keepdims=True)
        s_own = jnp.sum(kn * qcol, axis=0, keepdims=True)
        own = jnp.logical_and(lane_1 // dec_seq == h, lane_1 % dec_seq <= qi)
        s_own = jnp.where(own, s_own, NEG_INF)
        m = jnp.maximum(jnp.max(s, axis=1, keepdims=True), jnp.max(s_own, axis=1, keepdims=True))
        p = jnp.exp(s - m)
        p_own = jnp.exp(s_own - m)
        l = jnp.sum(p, axis=1, keepdims=True) + jnp.sum(p_own, axis=1, keepdims=True)
        o = jnp.sum(vbuf[slot, :, keys] * p, axis=1, keepdims=True) + jnp.sum(vn * p_own, axis=1, keepdims=True)
        out = jnp.where(lane_d == me, o * (1.0 / l), out)
    o_ref[0] = out


def _sample_attend(page_table, sel_packed, wq, kn_t, vn_t, ck_view, cv_view, dec_seq):
    n_batch = page_table.shape[0]
    n_keys = dec_seq * MOBA_TOPK * MOBA_BLOCK
    nblk = pl.BlockSpec((1, HEAD_DIM, LANES), lambda b, h, pt, sel: (b, 0, 0))
    return pl.pallas_call(
        functools.partial(_sample_attend_kernel, n_batch=n_batch, dec_seq=dec_seq),
        grid_spec=pltpu.PrefetchScalarGridSpec(
            num_scalar_prefetch=2,
            grid=(n_batch, N_HEADS),
            in_specs=[nblk, nblk, nblk, pl.BlockSpec(memory_space=pl.ANY), pl.BlockSpec(memory_space=pl.ANY)],
            out_specs=nblk,
            scratch_shapes=[pltpu.VMEM((ATTEND_SLOTS, HEAD_DIM, n_keys), F32),
                            pltpu.VMEM((ATTEND_SLOTS, HEAD_DIM, n_keys), F32),
                            pltpu.SemaphoreType.DMA((ATTEND_SLOTS, 2))],
        ),
        out_shape=jax.ShapeDtypeStruct((n_batch, HEAD_DIM, LANES), F32),
        compiler_params=pltpu.CompilerParams(dimension_semantics=("arbitrary", "arbitrary"),
                                             vmem_limit_bytes=VMEM_LIMIT),
        name="sample_attend",
    )(page_table, sel_packed, wq, kn_t, vn_t, ck_view, cv_view)


def kernel(x_prompt, x_sample, cache_k, cache_v, state_wkv, state_shift, page_table, n1_pre, n1_post, n2_pre, n2_post, w_in, rw_mu, rw_w0, rw_w2, rw_a0, rw_a2, rw_g2, rw_kk, rw_ka, rw_rk, rw_ln_w, rw_ln_b, w_att_br, w_rw_br, w_out, w_up, w_down):
    _, seq, _ = x_prompt.shape
    n_batch, dec_seq, _ = x_sample.shape
    n_pages = page_table.shape[1]
    past = n_pages * PAGE_SIZE
    assert past % MOBA_BLOCK == 0 and n_pages // PAGES_PER_BLOCK >= MOBA_TOPK
    assert n_pages // PAGES_PER_BLOCK <= (1 << SEL_BITS) and x_prompt.shape[0] == 1 and n1_pre.shape[0] == 1
    l = 0
    row2 = lambda a: a[l].reshape(1, -1)
    rwp = dict(rw_mu=row2(rw_mu), rw_w0=row2(rw_w0), rw_w2=rw_w2[l], rw_a0=row2(rw_a0), rw_a2=rw_a2[l],
               rw_g2=rw_g2[l], rw_kk=row2(rw_kk), rw_ka=row2(rw_ka), rw_rk=row2(rw_rk),
               rw_ln_w=row2(rw_ln_w), rw_ln_b=row2(rw_ln_b))
    w_in_b = w_in[l].astype(BF16)
    ffn_w = [w.astype(BF16) for w in (w_att_br[l], w_rw_br[l], w_out[l], w_up[l], w_down[l])]
    norms = [row2(n1_post), row2(n2_pre), row2(n2_post)]
    heads_t = lambda t: t.reshape(N_HEADS, HEAD_DIM, -1)

    xs = x_sample.reshape(n_batch * dec_seq, D_MODEL)
    cos_s, sin_s = _rope_tables(jnp.tile(past + jnp.arange(dec_seq), n_batch))
    qs, _, _, kt_s, vt_s, rw_s, gate_s, _ = _project(xs, cos_s, sin_s, row2(n1_pre), w_in_b)
    kn_t = heads_t(kt_s).reshape(N_HEADS, HEAD_DIM, n_batch, dec_seq)
    vn_t = heads_t(vt_s).reshape(N_HEADS, HEAD_DIM, n_batch, dec_seq)
    q4 = qs.reshape(n_batch, dec_seq, N_HEADS, HEAD_DIM)
    q_t = q4.transpose(0, 3, 2, 1).reshape(n_batch, HEAD_DIM, N_HEADS * dec_seq)
    q_t = jnp.pad(q_t, ((0, 0), (0, 0), (0, LANES - N_HEADS * dec_seq)))
    ck_view = jnp.swapaxes(cache_k[:, l], -1, -2)
    cv_view = jnp.swapaxes(cache_v[:, l], -1, -2)

    xp = x_prompt[0]
    cos, sin = _rope_tables(jnp.arange(seq))
    q, kb, vtb, kt, vt, rw, gate, kmean = _project(xp, cos, sin, row2(n1_pre), w_in_b)
    n_blk = seq // MOBA_BLOCK
    y_att = _moba_prompt(q, kb.reshape(n_blk, MOBA_BLOCK, ATT_W), vtb, kmean.reshape(n_blk, ATT_W))
    y_rw, wkv_p, sel = _rwkv_prompt_and_select(rw, rwp, page_table, q_t, ck_view, dec_seq)
    y_prompt = _merge_ffn(xp, y_att, y_rw, gate, *ffn_w, *norms)
    k_prompt = heads_t(kt).transpose(0, 2, 1)[None, None]
    v_prompt = heads_t(vt).transpose(0, 2, 1)[None, None]

    sel = sel[:, :MOBA_TOPK, :N_HEADS * dec_seq]
    sel_packed = sel[:, 0] | (sel[:, 1] << SEL_BITS) | (sel[:, 2] << (2 * SEL_BITS))
    pad_new = lambda t: jnp.pad(t.transpose(2, 1, 0, 3).reshape(n_batch, HEAD_DIM, N_HEADS * dec_seq),
                                ((0, 0), (0, 0), (0, LANES - N_HEADS * dec_seq)))
    o_s = _sample_attend(page_table, sel_packed, q_t * (HEAD_DIM ** -0.5), pad_new(kn_t), pad_new(vn_t),
                         ck_view, cv_view, dec_seq)
    o_s = o_s[:, :, :N_HEADS * dec_seq].reshape(n_batch, HEAD_DIM, N_HEADS, dec_seq)
    y_att_s = o_s.transpose(0, 3, 2, 1).reshape(n_batch * dec_seq, ATT_W).astype(BF16)

    rw_s3 = rw_s.reshape(n_batch, dec_seq, SHIFT_W)
    prev_s = jnp.concatenate([state_shift[:, l][:, None], rw_s3[:, :-1]], axis=1).reshape(n_batch * dec_seq, SHIFT_W)
    to_steps = lambda t: t.reshape(n_batch, dec_seq, RW_HEADS, RW_N).transpose(1, 2, 3, 0)
    vecs = [to_steps(t) for t in _rwkv_mix_call(rw_s, prev_s, rwp)]
    on_lanes = lambda a: jnp.broadcast_to(a[l].reshape(RW_HEADS, RW_N, 1), (RW_HEADS, RW_N, n_batch))
    y_steps, wkv_s = _rwkv_steps(vecs, state_wkv[:, l].transpose(1, 2, 3, 0),
                                 on_lanes(rw_ln_w), on_lanes(rw_ln_b), on_lanes(rw_rk))
    y_rw_s = y_steps.transpose(3, 0, 1, 2).reshape(n_batch * dec_seq, RW_W).astype(BF16)
    y_sample = _merge_ffn(xs, y_att_s, y_rw_s, gate_s, *ffn_w, *norms)
    k_sample = kn_t.transpose(2, 0, 3, 1)[:, None]
    v_sample = vn_t.transpose(2, 0, 3, 1)[:, None]

    return (y_prompt[None], y_sample.reshape(n_batch, dec_seq, D_MODEL), k_prompt, v_prompt, k_sample, v_sample,
            wkv_p[None, None], wkv_s.transpose(3, 0, 1, 2)[:, None], rw[-1][None, None], rw_s3[:, -1][:, None])
```
